```python
import math
import jax, jax.numpy as jnp
from jax import lax
import numpy as np

D_MODEL = 4096
BATCH = 4
SEQ = 4096
DEPTH = 1

N_META = 16
D_S5 = D_MODEL // 2
S5_GROUP = 16
S5_GROUPS = D_S5 // S5_GROUP
S5_STATE = 64
D_HG = D_MODEL // 2
HG_EXPAND = 128
HG_HEADS = D_HG // HG_EXPAND
HG_CHUNK = 64
N_EXPERT_GROUPS = 8
EXPERTS_PER_GROUP = 8
N_EXPERTS = N_EXPERT_GROUPS * EXPERTS_PER_GROUP
TOP_K_INNER = 2
D_EXPERT = D_MODEL // 8
MOE_BLOCK = 128
RMS_EPS = 1e-6
IN_SIZES = (D_S5, D_HG, D_HG, D_HG, D_HG, D_HG, D_MODEL, D_MODEL)
IN_COLS = sum(IN_SIZES)
IN_SPLITS = tuple(int(s) for s in np.cumsum(IN_SIZES)[:-1])

kernel_name = "hybrid_s5_hgrn2_hmoe_encoder"


def rmsnorm(x, w):
    xf = x.astype(jnp.float32)
    y = xf * lax.rsqrt(jnp.mean(xf * xf, axis=-1, keepdims=True) + RMS_EPS)
    return (y * w.astype(jnp.float32)).astype(x.dtype)


def cmul(ar, ai, br, bi):
    return ar * br - ai * bi, ar * bi + ai * br


def s5_direction(u, a_re, a_im, log_dt, b_re, b_im, c_re, c_im, reverse):
    dt = jnp.exp(log_dt)[:, None]
    mag = jnp.exp(a_re * dt)
    ab_re, ab_im = mag * jnp.cos(a_im * dt), mag * jnp.sin(a_im * dt)
    den = a_re * a_re + a_im * a_im
    nr, ni = ab_re - 1.0, ab_im
    z_re = (nr * a_re + ni * a_im) / den
    z_im = (ni * a_re - nr * a_im) / den
    bb_re, bb_im = cmul(z_re[..., None], z_im[..., None], b_re, b_im)
    bu_re = jnp.einsum('blgh,gph->blgp', u, bb_re)
    bu_im = jnp.einsum('blgh,gph->blgp', u, bb_im)
    L = u.shape[1]
    a_r = jnp.broadcast_to(ab_re, (1, L) + ab_re.shape)
    a_i = jnp.broadcast_to(ab_im, (1, L) + ab_im.shape)

    def combine(e1, e2):
        a1r, a1i, b1r, b1i = e1
        a2r, a2i, b2r, b2i = e2
        ar, ai = cmul(a2r, a2i, a1r, a1i)
        tr, ti = cmul(a2r, a2i, b1r, b1i)
        return ar, ai, tr + b2r, ti + b2i

    _, _, xr, xi = lax.associative_scan(combine, (a_r, a_i, bu_re, bu_im), reverse=reverse, axis=1)
    return jnp.einsum('blgp,ghp->blgh', xr, c_re) - jnp.einsum('blgp,ghp->blgh', xi, c_im)


def gla_chunkwise(q, k, v, logg):
    B, H, T, K = q.shape
    V = v.shape[-1]
    n = T // HG_CHUNK

    def to_chunks(t):
        return jnp.moveaxis(t.reshape(B, H, n, HG_CHUNK, t.shape[-1]), 2, 0)

    incl = jnp.tril(jnp.ones((HG_CHUNK, HG_CHUNK), dtype=bool))[:, :, None]

    def step(S, inp):
        qc, kc, vc, gc = inp
        b = jnp.cumsum(gc, axis=2)
        b_last = b[:, :, -1:, :]
        o_inter = jnp.einsum('bhtk,bhkv->bhtv', qc * jnp.exp(b), S)
        decay = jnp.exp(jnp.where(incl, b[:, :, :, None, :] - b[:, :, None, :, :], -jnp.inf))
        scores = jnp.einsum('bhtk,bhsk,bhtsk->bhts', qc, kc, decay)
        o_intra = jnp.einsum('bhts,bhsv->bhtv', scores, vc)
        S = jnp.exp(b_last[:, :, 0, :])[..., None] * S + jnp.einsum('bhsk,bhsv->bhkv', kc * jnp.exp(b_last - b), vc)
        return S, o_inter + o_intra

    S0 = jnp.zeros((B, H, K, V), jnp.float32)
    _, o = lax.scan(step, S0, (to_chunks(q), to_chunks(k), to_chunks(v), to_chunks(logg)))
    return jnp.moveaxis(o, 0, 2).reshape(B, H, T, V)


def hybrid_mixer(a, layer, w_in, s5_a_re, s5_a_im, s5_log_dt, s5_b_re, s5_b_im, s5_c_re, s5_c_im,
                 s5_d, s5_glu_w, s5_glu_b, hg_lb_gamma, hg_norm_w, w_branch_a, w_branch_b, w_out):
    B, L, _ = a.shape
    f32 = jnp.float32
    proj = jnp.einsum('bld,de->ble', a, w_in)
    u, q, i_in, f_fw, f_bw, g_out, gate_a, gate_b = jnp.split(proj, IN_SPLITS, axis=-1)

    uf = u.astype(f32).reshape(B, L, S5_GROUPS, S5_GROUP)
    y = uf.reshape(B, L, D_S5) * s5_d.astype(f32)
    for d in range(2):
        y = y + s5_direction(uf, s5_a_re[d].astype(f32), s5_a_im[d].astype(f32), s5_log_dt[d].astype(f32),
                             s5_b_re[d].astype(f32), s5_b_im[d].astype(f32), s5_c_re[d].astype(f32),
                             s5_c_im[d].astype(f32), reverse=(d == 1)).reshape(B, L, D_S5)
    z = jax.nn.gelu(y)
    s5_out = z * jax.nn.sigmoid(z @ s5_glu_w.astype(f32) + s5_glu_b.astype(f32))

    lb = jnp.cumsum(jax.nn.softmax(hg_lb_gamma.astype(f32), axis=1), axis=1)[:, layer]
    n_pad = HG_CHUNK - N_META

    def heads(t):
        return t.reshape(B, L, HG_HEADS, HG_EXPAND).transpose(0, 2, 1, 3)

    def pad(t):
        return jnp.pad(t, ((0, 0), (0, 0), (n_pad, 0), (0, 0)))

    qh = pad(heads(jax.nn.silu(q.astype(f32))))
    vh = pad(heads(i_in.astype(f32)))
    o = jnp.zeros(qh.shape, f32)
    for d, f_logit in enumerate((f_fw, f_bw)):
        f = lb[d] + (1.0 - lb[d]) * jax.nn.sigmoid(f_logit.astype(f32))
        kh, gh = pad(heads(1.0 - f)), pad(heads(jnp.log(f)))
        if d == 0:
            o = o + gla_chunkwise(qh, kh, vh, gh)
        else:
            o = o + jnp.flip(gla_chunkwise(jnp.flip(qh, 2), jnp.flip(kh, 2), jnp.flip(vh, 2), jnp.flip(gh, 2)), 2)
    o = o[:, :, n_pad:]
    o = o * lax.rsqrt(jnp.mean(o * o, axis=-1, keepdims=True) + RMS_EPS)
    o = o.transpose(0, 2, 1, 3).reshape(B, L, D_HG)
    hg_out = o * hg_norm_w.astype(f32) * jax.nn.silu(g_out.astype(f32))

    y_a = jnp.einsum('blc,cd->bld', s5_out.astype(a.dtype), w_branch_a)
    y_b = jnp.einsum('blc,cd->bld', hg_out.astype(a.dtype), w_branch_b)
    m = jax.nn.sigmoid(gate_a) * y_a + jax.nn.sigmoid(gate_b) * y_b
    return jnp.einsum('bld,de->ble', m, w_out)


def expert_dispatch(xf, e_idx, gate, w1, w3, w2):
    N, K = e_idx.shape
    E = w1.shape[0]
    M = N * K
    flat_e = e_idx.reshape(M)
    flat_tok = jnp.repeat(jnp.arange(N, dtype=jnp.int32), K)
    flat_w = gate.reshape(M)
    order = jnp.argsort(flat_e, stable=True)
    se, stok, sw = flat_e[order], flat_tok[order], flat_w[order]
    counts = jnp.bincount(flat_e, length=E)
    padded = ((counts + MOE_BLOCK - 1) // MOE_BLOCK) * MOE_BLOCK
    pad_end = jnp.cumsum(padded)
    pad_start = pad_end - padded
    grp_start = jnp.cumsum(counts) - counts
    dest = pad_start[se] + jnp.arange(M, dtype=jnp.int32) - grp_start[se]
    n_blocks = -(-M // MOE_BLOCK) + E
    P = n_blocks * MOE_BLOCK
    buf_tok = jnp.zeros((P,), jnp.int32).at[dest].set(stok)
    buf_w = jnp.zeros((P,), jnp.float32).at[dest].set(sw)
    blk_e = jnp.minimum(jnp.searchsorted(pad_end, jnp.arange(n_blocks) * MOE_BLOCK, side='right'), E - 1)

    def run_block(args):
        tok, w, e = args
        xb = xf[tok]
        hb = jax.nn.silu(xb @ w1[e]) * (xb @ w3[e])
        return (hb @ w2[e]) * w[:, None].astype(xf.dtype)

    yb = lax.map(run_block, (buf_tok.reshape(n_blocks, MOE_BLOCK), buf_w.reshape(n_blocks, MOE_BLOCK), blk_e))
    return jnp.zeros_like(xf).at[buf_tok].add(yb.reshape(P, -1))


def hierarchical_moe(h, rg_w, rg_b, re_w, re_b, w1, w3, w2):
    B, L, D = h.shape
    xf = h.reshape(B * L, D)
    hf = xf.astype(jnp.float32)
    p_group = jax.nn.softmax(hf @ rg_w.astype(jnp.float32) + rg_b.astype(jnp.float32), axis=-1)
    g_prob, g_sel = lax.top_k(p_group, 1)
    logits_e = (hf @ re_w.astype(jnp.float32) + re_b.astype(jnp.float32)).reshape(-1, N_EXPERT_GROUPS, EXPERTS_PER_GROUP)
    logits_in = jnp.take_along_axis(logits_e, g_sel[:, :, None], axis=1)[:, 0]
    top_v, top_i = lax.top_k(logits_in, TOP_K_INNER)
    gate = jax.nn.softmax(top_v, axis=-1) * g_prob
    e_idx = g_sel * EXPERTS_PER_GROUP + top_i
    return expert_dispatch(xf, e_idx, gate, w1, w3, w2).reshape(B, L, D)


def setup_inputs(seed: int = 0) -> dict:
    key = jax.random.key(seed)
    ks = jax.random.split(key, 32)
    f32 = jnp.float32

    def nrm(k, shape, scale):
        return jax.random.normal(k, shape, f32) * scale

    G, P, H = S5_GROUPS, S5_STATE, S5_GROUP
    n_idx = jnp.arange(P, dtype=f32)
    return {
        "x": nrm(ks[0], (BATCH, SEQ, D_MODEL), 1.0),
        "meta_tokens": nrm(ks[1], (N_META, D_MODEL), 1.0),
        "norm_mix_w": 1.0 + nrm(ks[2], (DEPTH, D_MODEL), 0.01),
        "w_in": nrm(ks[3], (DEPTH, D_MODEL, IN_COLS), D_MODEL ** -0.5),
        "s5_a_re": -0.5 + nrm(ks[4], (DEPTH, 2, G, P), 0.01),
        "s5_a_im": math.pi * n_idx + nrm(ks[5], (DEPTH, 2, G, P), 0.01),
        "s5_log_dt": jax.random.uniform(ks[6], (DEPTH, 2, G), f32, math.log(1e-3), math.log(1e-1)),
        "s5_b_re": nrm(ks[7], (DEPTH, 2, G, P, H), (2 * H) ** -0.5),
        "s5_b_im": nrm(ks[8], (DEPTH, 2, G, P, H), (2 * H) ** -0.5),
        "s5_c_re": nrm(ks[9], (DEPTH, 2, G, H, P), (2 * P) ** -0.5),
        "s5_c_im": nrm(ks[10], (DEPTH, 2, G, H, P), (2 * P) ** -0.5),
        "s5_d": nrm(ks[11], (DEPTH, D_S5), 1.0),
        "s5_glu_w": nrm(ks[12], (DEPTH, D_S5, D_S5), D_S5 ** -0.5),
        "s5_glu_b": nrm(ks[13], (DEPTH, D_S5), 0.01),
        "hg_lb_gamma": nrm(ks[14], (2, DEPTH + 1, D_HG), 0.1),
        "hg_norm_w": 1.0 + nrm(ks[15], (DEPTH, D_HG), 0.01),
        "w_branch_a": nrm(ks[16], (DEPTH, D_S5, D_MODEL), D_S5 ** -0.5),
        "w_branch_b": nrm(ks[17], (DEPTH, D_HG, D_MODEL), D_HG ** -0.5),
        "w_out": nrm(ks[18], (DEPTH, D_MODEL, D_MODEL), D_MODEL ** -0.5),
        "norm_ffn_w": 1.0 + nrm(ks[19], (DEPTH, D_MODEL), 0.01),
        "router_group_w": nrm(ks[20], (DEPTH, D_MODEL, N_EXPERT_GROUPS), D_MODEL ** -0.5),
        "router_group_b": nrm(ks[21], (DEPTH, N_EXPERT_GROUPS), 0.01),
        "router_expert_w": nrm(ks[22], (DEPTH, D_MODEL, N_EXPERTS), D_MODEL ** -0.5),
        "router_expert_b": nrm(ks[23], (DEPTH, N_EXPERTS), 0.01),
        "expert_w1": nrm(ks[24], (DEPTH, N_EXPERTS, D_MODEL, D_EXPERT), D_MODEL ** -0.5),
        "expert_w3": nrm(ks[25], (DEPTH, N_EXPERTS, D_MODEL, D_EXPERT), D_MODEL ** -0.5),
        "expert_w2": nrm(ks[26], (DEPTH, N_EXPERTS, D_EXPERT, D_MODEL), D_EXPERT ** -0.5),
        "norm_final_w": 1.0 + nrm(ks[27], (D_MODEL,), 0.01),
    }


def reference(x, meta_tokens, norm_mix_w, w_in, s5_a_re, s5_a_im, s5_log_dt, s5_b_re, s5_b_im, s5_c_re,
              s5_c_im, s5_d, s5_glu_w, s5_glu_b, hg_lb_gamma, hg_norm_w, w_branch_a, w_branch_b, w_out,
              norm_ffn_w, router_group_w, router_group_b, router_expert_w, router_expert_b,
              expert_w1, expert_w3, expert_w2, norm_final_w):
    B = x.shape[0]
    meta = jnp.broadcast_to(meta_tokens.astype(x.dtype)[None], (B, N_META, x.shape[-1]))
    h = jnp.concatenate([meta, x], axis=1)
    for l in range(DEPTH):
        a = rmsnorm(h, norm_mix_w[l])
        h = h + hybrid_mixer(a, l, w_in[l], s5_a_re[l], s5_a_im[l], s5_log_dt[l], s5_b_re[l], s5_b_im[l],
                             s5_c_re[l], s5_c_im[l], s5_d[l], s5_glu_w[l], s5_glu_b[l], hg_lb_gamma,
                             hg_norm_w[l], w_branch_a[l], w_branch_b[l], w_out[l])
        f = rmsnorm(h, norm_ffn_w[l])
        h = h + hierarchical_moe(f, router_group_w[l], router_group_b[l], router_expert_w[l], router_expert_b[l],
                                 expert_w1[l], expert_w3[l], expert_w2[l])
    h = rmsnorm(h, norm_final_w)
    return h[:, N_META:]
```

```python
import functools
import math

import jax
import jax.numpy as jnp
from jax import lax
from jax.experimental import pallas as pl
from jax.experimental.pallas import tpu as pltpu

F32 = jnp.float32
BF16 = jnp.bfloat16

N_META = 16
S5_GROUP = 16
S5_STATE = 64
S5_CHUNK = 16
HG_EXPAND = 128
GLA_CHUNK = 64
N_EXPERT_GROUPS = 8
EXPERTS_PER_GROUP = 8
TOP_K_INNER = 2
MOE_BLOCK = 128
RMS_EPS = 1e-6
LANES = 128
ROUTER_COLS = 128
VMEM_LIMIT = 56 * 1024 * 1024


def _cparams(sem):
    return pltpu.CompilerParams(dimension_semantics=sem, vmem_limit_bytes=VMEM_LIMIT)


def _tile(n, pref):
    t = min(n, pref)
    assert n % t == 0, (n, pref)
    return t


def _rmsnorm_kernel(x_ref, w_ref, o_ref):
    x = x_ref[...]
    y = x * lax.rsqrt(jnp.mean(x * x, axis=-1, keepdims=True) + RMS_EPS)
    o_ref[...] = (y * w_ref[...]).astype(o_ref.dtype)


def _rmsnorm(x, w, out_dtype=BF16):
    n, d = x.shape
    tm = _tile(n, 256)
    return pl.pallas_call(
        _rmsnorm_kernel,
        grid=(n // tm,),
        in_specs=[pl.BlockSpec((tm, d), lambda i: (i, 0)), pl.BlockSpec((1, d), lambda i: (0, 0))],
        out_specs=pl.BlockSpec((tm, d), lambda i: (i, 0)),
        out_shape=jax.ShapeDtypeStruct((n, d), out_dtype),
        compiler_params=_cparams(("parallel",)),
        name="rmsnorm",
    )(x, w.reshape(1, d).astype(F32))


def _mm_kernel(*refs, n_pairs, n_extra, nk, epilogue, lane_blocked):
    a_refs = refs[0:2 * n_pairs:2]
    w_refs = refs[1:2 * n_pairs:2]
    extras = refs[2 * n_pairs:2 * n_pairs + n_extra]
    o_ref = refs[2 * n_pairs + n_extra]
    acc_ref = refs[2 * n_pairs + n_extra + 1]
    k = pl.program_id(3)

    @pl.when(k == 0)
    def _():
        acc_ref[...] = jnp.zeros_like(acc_ref)

    acc = acc_ref[...]
    for a_ref, w_ref in zip(a_refs, w_refs):
        acc += jnp.dot(a_ref[...].astype(BF16), w_ref[...].astype(BF16), preferred_element_type=F32)
    acc_ref[...] = acc

    @pl.when(k == nk - 1)
    def _():
        res = epilogue(acc_ref[...], *[e[...] for e in extras]).astype(o_ref.dtype)
        if lane_blocked:
            for q in range(o_ref.shape[0]):
                o_ref[q] = res[:, q * LANES:(q + 1) * LANES]
        else:
            o_ref[...] = res


def _matmul(pairs, *, n_cols, col_off=0, extras=(), epilogue=None, out_dtype=BF16, lane_blocked=False,
            tm=1024, tn=1024, tk=512, name="matmul"):
    if epilogue is None:
        epilogue = lambda acc: acc
    a0, w0 = pairs[0]
    batched = a0.ndim == 3
    g = a0.shape[0] if batched else 1
    m, kdim = a0.shape[-2:]
    tm, tn, tk = _tile(m, tm), _tile(math.gcd(n_cols, col_off) if col_off else n_cols, tn), _tile(kdim, tk)
    cb = col_off // tn
    nk = kdim // tk
    lead = (None,) if batched else ()

    def bidx(b):
        return (b,) if batched else ()

    in_specs, args = [], []
    for a, w in pairs:
        in_specs.append(pl.BlockSpec(lead + (tm, tk), lambda b, i, j, k: bidx(b) + (i, k)))
        in_specs.append(pl.BlockSpec(lead + (tk, tn), lambda b, i, j, k: bidx(b) + (k, j + cb)))
        args += [a, w]
    for e in extras:
        if e.shape[-2] == 1:
            in_specs.append(pl.BlockSpec((1, tn), lambda b, i, j, k: (0, j)))
        else:
            in_specs.append(pl.BlockSpec((tm, tn), lambda b, i, j, k: (i, j)))
        args.append(e)
    if lane_blocked:
        assert not batched
        out_shape = jax.ShapeDtypeStruct((n_cols // LANES, m, LANES), out_dtype)
        out_spec = pl.BlockSpec((tn // LANES, tm, LANES), lambda b, i, j, k: (j, i, 0))
    else:
        out_shape = jax.ShapeDtypeStruct(((g,) if batched else ()) + (m, n_cols), out_dtype)
        out_spec = pl.BlockSpec(lead + (tm, tn), lambda b, i, j, k: bidx(b) + (i, j))
    return pl.pallas_call(
        functools.partial(_mm_kernel, n_pairs=len(pairs), n_extra=len(extras), nk=nk, epilogue=epilogue,
                          lane_blocked=lane_blocked),
        grid=(g, m // tm, n_cols // tn, nk),
        in_specs=in_specs,
        out_specs=out_spec,
        out_shape=out_shape,
        scratch_shapes=[pltpu.VMEM((tm, tn), F32)],
        compiler_params=_cparams(("parallel", "parallel", "parallel", "arbitrary")),
        name=name,
    )(*args)


def _s5_matrices(a_re, a_im, log_dt, b_re, b_im, c_re, c_im, d_skip):
    T = S5_CHUNK
    hp = lax.Precision.HIGHEST
    G, P = a_re.shape[1:]
    H = b_re.shape[-1]
    GB = LANES // H
    J = G // GB
    dt = jnp.exp(log_dt)[..., None]
    lre, lang = a_re * dt, a_im * dt
    mag = jnp.exp(lre)
    ab_re, ab_im = mag * jnp.cos(lang), mag * jnp.sin(lang)
    den = a_re * a_re + a_im * a_im
    nr, ni = ab_re - 1.0, ab_im
    z_re = ((nr * a_re + ni * a_im) / den)[..., None]
    z_im = ((ni * a_re - nr * a_im) / den)[..., None]
    bb_re = z_re * b_re - z_im * b_im
    bb_im = z_re * b_im + z_im * b_re
    n = jnp.arange(T + 1, dtype=F32)
    pmag = jnp.exp(lre[..., None] * n)
    pw_re = pmag * jnp.cos(lang[..., None] * n)
    pw_im = pmag * jnp.sin(lang[..., None] * n)
    cl_re = c_re[..., None] * pw_re[:, :, None] - c_im[..., None] * pw_im[:, :, None]
    cl_im = c_re[..., None] * pw_im[:, :, None] + c_im[..., None] * pw_re[:, :, None]
    kk = (jnp.einsum('dgqpn,dgph->dgnqh', cl_re, bb_re, precision=hp)
          - jnp.einsum('dgqpn,dgph->dgnqh', cl_im, bb_im, precision=hp))
    s_idx = jnp.arange(T)[:, None]
    t_idx = jnp.arange(T)[None, :]
    kf = jnp.where((t_idx >= s_idx)[None, :, :, None, None], kk[0][:, jnp.clip(t_idx - s_idx, 0, T)], 0.0)
    kb = jnp.where((s_idx >= t_idx)[None, :, :, None, None], kk[1][:, jnp.clip(s_idx - t_idx, 0, T)], 0.0)
    eye_t = jnp.eye(T, dtype=F32)
    eye_h = jnp.eye(H, dtype=F32)
    dd = d_skip.reshape(G, H)
    kt = kf + kb + eye_t[None, :, :, None, None] * (eye_h[None] * dd[:, :, None])[:, None, None]
    eye_g = jnp.eye(GB, dtype=F32)
    kt = kt.reshape(J, GB, T, T, H, H)
    m_mat = jnp.einsum('jgstqh,gk->jsghtkq', kt, eye_g).reshape(J, T * LANES, T * LANES)

    def pw_at(d, idx):
        return pw_re[d][:, :, idx], pw_im[d][:, :, idx]
    ar = jnp.arange(T)
    e_parts = []
    for d, idx in ((0, T - 1 - ar), (1, ar)):
        pr, pi = pw_at(d, idx)
        e_re = pr[..., None] * bb_re[d][:, :, None, :] - pi[..., None] * bb_im[d][:, :, None, :]
        e_im = pr[..., None] * bb_im[d][:, :, None, :] + pi[..., None] * bb_re[d][:, :, None, :]
        e_parts += [e_re, e_im]
    ee = jnp.stack(e_parts, 0).reshape(4, J, GB, P, T, H)
    e_mat = jnp.einsum('djgpsh,gk->jsghdkp', ee, eye_g).reshape(J, T * LANES, 4 * GB * P)
    f_parts = []
    for d, idx in ((0, ar + 1), (1, T - ar)):
        f_parts += [cl_re[d][..., idx], -cl_im[d][..., idx]]
    ff = jnp.stack(f_parts, 0).reshape(4, J, GB, H, P, T)
    f_mat = jnp.einsum('djgqpt,gk->jdgptkq', ff, eye_g).reshape(J, 4 * GB * P, T * LANES)
    lam_t = jnp.stack([pw_re[0][..., T], pw_im[0][..., T], pw_re[1][..., T], pw_im[1][..., T]], 0)
    lam_t = lam_t.reshape(4, J, GB * P).transpose(1, 0, 2).reshape(J, 1, 4 * GB * P)
    return e_mat.astype(BF16), m_mat.astype(BF16), f_mat.astype(BF16), lam_t


def _s5_scan_kernel(s_ref, lam_ref, x0_ref, o_ref, xp_ref, *, n_batch, n_chunks):
    hw = s_ref.shape[-1] // 4
    lam = lam_ref[...]
    lfr, lfi, lbr, lbi = (lam[:, q * hw:(q + 1) * hw] for q in range(4))
    x0 = x0_ref[...]
    zeros = jnp.zeros((1, hw), F32)
    init = tuple((x0[:, 0:hw], x0[:, hw:2 * hw], zeros, zeros) for _ in range(n_batch))

    def step(c, carry):
        out = []
        for b, (xfr, xfi, xbr, xbi) in enumerate(carry):
            row_f = pl.ds(b * n_chunks + c, 1)
            row_b = pl.ds(b * n_chunks + n_chunks - 1 - c, 1)
            xp_ref[row_f, 0:hw] = xfr
            xp_ref[row_f, hw:2 * hw] = xfi
            xp_ref[row_b, 2 * hw:3 * hw] = xbr
            xp_ref[row_b, 3 * hw:4 * hw] = xbi
            sf = s_ref[row_f, 0:2 * hw]
            sb = s_ref[row_b, 2 * hw:4 * hw]
            out.append((lfr * xfr - lfi * xfi + sf[:, 0:hw],
                        lfr * xfi + lfi * xfr + sf[:, hw:2 * hw],
                        lbr * xbr - lbi * xbi + sb[:, 0:hw],
                        lbr * xbi + lbi * xbr + sb[:, hw:2 * hw]))
        return tuple(out)

    lax.fori_loop(0, n_chunks, step, init)
    o_ref[...] = xp_ref[...].astype(o_ref.dtype)


def _s5_scan(s_all, lam_t, x0, n_batch):
    j, r, w = s_all.shape
    n_chunks = r // n_batch
    return pl.pallas_call(
        functools.partial(_s5_scan_kernel, n_batch=n_batch, n_chunks=n_chunks),
        grid=(j,),
        in_specs=[pl.BlockSpec((None, r, w), lambda i: (i, 0, 0)),
                  pl.BlockSpec((None, 1, w), lambda i: (i, 0, 0)),
                  pl.BlockSpec((None, 1, w), lambda i: (i, 0, 0))],
        out_specs=pl.BlockSpec((None, r, w), lambda i: (i, 0, 0)),
        out_shape=jax.ShapeDtypeStruct((j, r, w), BF16),
        scratch_shapes=[pltpu.VMEM((r, w), F32)],
        compiler_params=_cparams(("parallel",)),
        name="s5_scan",
    )(s_all, lam_t, x0)


def _glu_kernel(y_ref, w_ref, b_ref, o_ref, z_ref):
    jn = pl.program_id(1)
    tn = o_ref.shape[-1]

    @pl.when(jn == 0)
    def _():
        for q in range(y_ref.shape[0]):
            z_ref[:, q * LANES:(q + 1) * LANES] = jax.nn.gelu(y_ref[q].astype(F32)).astype(z_ref.dtype)

    gate = jnp.dot(z_ref[...], w_ref[...].astype(BF16), preferred_element_type=F32) + b_ref[...]
    zc = z_ref[:, pl.ds(pl.multiple_of(jn * tn, tn), tn)].astype(F32)
    o_ref[...] = (zc * jax.nn.sigmoid(gate)).astype(o_ref.dtype)


def _glu(y_blk, glu_w, glu_b):
    jb, n, _ = y_blk.shape
    d = jb * LANES
    tm, tn = _tile(n, 1024), _tile(d, 1024)
    return pl.pallas_call(
        _glu_kernel,
        grid=(n // tm, d // tn),
        in_specs=[pl.BlockSpec((jb, tm, LANES), lambda i, j: (0, i, 0)),
                  pl.BlockSpec((d, tn), lambda i, j: (0, j)),
                  pl.BlockSpec((1, tn), lambda i, j: (0, j))],
        out_specs=pl.BlockSpec((tm, tn), lambda i, j: (i, j)),
        out_shape=jax.ShapeDtypeStruct((n, d), BF16),
        scratch_shapes=[pltpu.VMEM((tm, d), BF16)],
        compiler_params=_cparams(("parallel", "arbitrary")),
        name="s5_glu",
    )(y_blk, glu_w, glu_b.reshape(1, d).astype(F32))


def _nt(a, b):
    return lax.dot_general(a, b, (((1,), (1,)), ((), ())), preferred_element_type=F32)


def _gla_kernel(q_ref, v_ref, zf_ref, zb_ref, go_ref, vm_ref, zm_ref, lb_ref, nw_ref, o_ref, acc_ref):
    L, K = q_ref.shape
    C = GLA_CHUNK
    hp = lax.Precision.HIGHEST
    lb = lb_ref[...]
    lbf, lbb = lb[0:1, :], lb[1:2, :]
    row = lax.broadcasted_iota(jnp.int32, (C, C), 0)
    col = lax.broadcasted_iota(jnp.int32, (C, C), 1)
    lower = row >= col
    tril = lower.astype(F32)
    triu = (col >= row).astype(F32)

    def gates(z, lbd):
        f = lbd + (1.0 - lbd) * jax.nn.sigmoid(z)
        return 1.0 - f, jnp.log(f)

    def chunk(c, st, z_ref, lbd, cum, keep, ref_row, edge_row, first):
        rows = pl.ds(pl.multiple_of(c * C, C), C)
        k, g = gates(z_ref[rows, :], lbd)
        b = jnp.dot(cum, g, precision=hp, preferred_element_type=F32)
        b_ref_row = b[ref_row:ref_row + 1, :]
        b_edge = b[edge_row:edge_row + 1, :]
        q = q_ref[rows, :].astype(F32)
        v = v_ref[rows, :]
        qt = (q * jnp.exp(b - b_ref_row)).astype(BF16)
        kt = (k * jnp.exp(b_ref_row - b)).astype(BF16)
        s = jnp.where(keep, _nt(qt, kt), 0.0).astype(BF16)
        qe = (q * jnp.exp(b)).astype(BF16)
        o = jnp.dot(s, v, preferred_element_type=F32) + _nt(qe, st.astype(BF16))
        if first:
            acc_ref[rows, :] = o
        else:
            acc_ref[rows, :] += o
        kh = (k * jnp.exp(b_edge - b)).astype(BF16)
        vt = v.astype(F32).T.astype(BF16)
        return st * jnp.exp(b_edge) + jnp.dot(vt, kh, preferred_element_type=F32)

    m = zm_ref.shape[0]
    km, gm = gates(zm_ref[...], lbf)
    mrow = lax.broadcasted_iota(jnp.int32, (m, m), 0)
    mcol = lax.broadcasted_iota(jnp.int32, (m, m), 1)
    bm = jnp.dot((mrow >= mcol).astype(F32), gm, precision=hp, preferred_element_type=F32)
    khm = (km * jnp.exp(bm[m - 1:m, :] - bm)).astype(BF16)
    st0 = jnp.dot(vm_ref[...].astype(F32).T.astype(BF16), khm, preferred_element_type=F32)

    n_chunks = L // C
    lax.fori_loop(0, n_chunks,
                  lambda c, st: chunk(c, st, zf_ref, lbf, tril, lower, C // 2 - 1, C - 1, True), st0)
    lax.fori_loop(0, n_chunks,
                  lambda i, st: chunk(n_chunks - 1 - i, st, zb_ref, lbb, triu, col >= row, C // 2, 0, False),
                  jnp.zeros((K, K), F32))
    o = acc_ref[...]
    o = o * lax.rsqrt(jnp.mean(o * o, axis=-1, keepdims=True) + RMS_EPS)
    o_ref[...] = (o * nw_ref[...] * go_ref[...].astype(F32)).astype(o_ref.dtype)


def _gla(q, v, zf, zb, go, v_meta, zf_meta, lb, norm_w, n_batch):
    n, d = q.shape
    L = n // n_batch
    heads = d // HG_EXPAND
    m = v_meta.shape[0]
    tok = pl.BlockSpec((L, HG_EXPAND), lambda b, h: (b, h))
    return pl.pallas_call(
        _gla_kernel,
        grid=(n_batch, heads),
        in_specs=[tok, tok, tok, tok, tok,
                  pl.BlockSpec((m, HG_EXPAND), lambda b, h: (0, h)),
                  pl.BlockSpec((m, HG_EXPAND), lambda b, h: (0, h)),
                  pl.BlockSpec((2, HG_EXPAND), lambda b, h: (0, h)),
                  pl.BlockSpec((1, HG_EXPAND), lambda b, h: (0, h))],
        out_specs=tok,
        out_shape=jax.ShapeDtypeStruct((n, d), BF16),
        scratch_shapes=[pltpu.VMEM((L, HG_EXPAND), F32)],
        compiler_params=_cparams(("parallel", "parallel")),
        name="hgrn2_gla",
    )(q, v, zf, zb, go, v_meta, zf_meta, lb, norm_w.reshape(1, d).astype(F32))


def _merge_kernel(a1_ref, w1_ref, a2_ref, w2_ref, g1_ref, g2_ref, o_ref, acc1_ref, acc2_ref, *, nk):
    k = pl.program_id(2)

    @pl.when(k == 0)
    def _():
        acc1_ref[...] = jnp.zeros_like(acc1_ref)
        acc2_ref[...] = jnp.zeros_like(acc2_ref)

    acc1_ref[...] += jnp.dot(a1_ref[...], w1_ref[...].astype(BF16), preferred_element_type=F32)
    acc2_ref[...] += jnp.dot(a2_ref[...], w2_ref[...].astype(BF16), preferred_element_type=F32)

    @pl.when(k == nk - 1)
    def _():
        o_ref[...] = (g1_ref[...].astype(F32) * acc1_ref[...]
                      + g2_ref[...].astype(F32) * acc2_ref[...]).astype(o_ref.dtype)


def _merge(a1, w1, a2, w2, g1, g2):
    n, kdim = a1.shape
    d = w1.shape[1]
    tm, tn, tk = _tile(n, 1024), _tile(d, 1024), _tile(kdim, 512)
    nk = kdim // tk
    a_spec = pl.BlockSpec((tm, tk), lambda i, j, k: (i, k))
    w_spec = pl.BlockSpec((tk, tn), lambda i, j, k: (k, j))
    o_spec = pl.BlockSpec((tm, tn), lambda i, j, k: (i, j))
    return pl.pallas_call(
        functools.partial(_merge_kernel, nk=nk),
        grid=(n // tm, d // tn, nk),
        in_specs=[a_spec, w_spec, a_spec, w_spec, o_spec, o_spec],
        out_specs=o_spec,
        out_shape=jax.ShapeDtypeStruct((n, d), BF16),
        scratch_shapes=[pltpu.VMEM((tm, tn), F32), pltpu.VMEM((tm, tn), F32)],
        compiler_params=_cparams(("parallel", "parallel", "arbitrary")),
        name="branch_merge",
    )(a1, w1, a2, w2, g1, g2)


def _router_kernel(h_ref, nw_ref, rw_ref, rb_ref, f_ref, lg_ref):
    x = h_ref[...]
    f = x * lax.rsqrt(jnp.mean(x * x, axis=-1, keepdims=True) + RMS_EPS) * nw_ref[...]
    f_ref[...] = f
    lg_ref[...] = jnp.dot(f, rw_ref[...], precision=lax.Precision.HIGHEST,
                          preferred_element_type=F32) + rb_ref[...]


def _router(h, norm_w, rw, rb):
    n, d = h.shape
    tm = _tile(n, 256)
    return pl.pallas_call(
        _router_kernel,
        grid=(n // tm,),
        in_specs=[pl.BlockSpec((tm, d), lambda i: (i, 0)), pl.BlockSpec((1, d), lambda i: (0, 0)),
                  pl.BlockSpec((d, ROUTER_COLS), lambda i: (0, 0)),
                  pl.BlockSpec((1, ROUTER_COLS), lambda i: (0, 0))],
        out_specs=[pl.BlockSpec((tm, d), lambda i: (i, 0)), pl.BlockSpec((tm, ROUTER_COLS), lambda i: (i, 0))],
        out_shape=[jax.ShapeDtypeStruct((n, d), F32), jax.ShapeDtypeStruct((n, ROUTER_COLS), F32)],
        compiler_params=_cparams(("parallel",)),
        name="ffn_norm_router",
    )(h, norm_w.reshape(1, d).astype(F32), rw, rb)


def _row_copy(src_hbm, dst_vmem, sem, src_row, slot, dst_row):
    return pltpu.make_async_copy(src_hbm.at[pl.ds(src_row, 1)], dst_vmem.at[slot, pl.ds(dst_row, 1)], sem.at[slot])


def _moe_kernel(blk_e_ref, n_used_ref, tok_ref, f_hbm, w1_ref, w3_ref, w2_ref, gw_ref, o_ref, xs_ref, sem):
    del blk_e_ref
    i = pl.program_id(0)
    n_used = n_used_ref[0]
    rows = xs_ref.shape[1]

    def start(blk, slot):
        def body(r, _):
            _row_copy(f_hbm, xs_ref, sem, tok_ref[blk * rows + r], slot, r).start()
            return 0
        lax.fori_loop(0, rows, body, 0)

    def wait(slot):
        def body(r, _):
            _row_copy(f_hbm, xs_ref, sem, 0, slot, r).wait()
            return 0
        lax.fori_loop(0, rows, body, 0)

    @pl.when(jnp.logical_and(i == 0, n_used > 0))
    def _():
        start(0, 0)

    @pl.when(i + 1 < n_used)
    def _():
        start(i + 1, (i + 1) % 2)

    @pl.when(i < n_used)
    def _():
        slot = i % 2
        wait(slot)
        x = xs_ref[slot].astype(BF16)
        h1 = jnp.dot(x, w1_ref[...], preferred_element_type=F32)
        h3 = jnp.dot(x, w3_ref[...], preferred_element_type=F32)
        hb = (jax.nn.silu(h1) * h3).astype(BF16)
        y = jnp.dot(hb, w2_ref[...], preferred_element_type=F32)
        o_ref[...] = (y * gw_ref[...]).astype(o_ref.dtype)

    @pl.when(i >= n_used)
    def _():
        o_ref[...] = jnp.zeros_like(o_ref)


def _moe(f, w1, w3, w2, blk_e, n_used, buf_tok, buf_w):
    n, d = f.shape
    e, _, hdim = w1.shape
    p = buf_tok.shape[0]
    n_blocks = p // MOE_BLOCK
    grid_spec = pltpu.PrefetchScalarGridSpec(
        num_scalar_prefetch=3,
        grid=(n_blocks,),
        in_specs=[pl.BlockSpec(memory_space=pl.ANY),
                  pl.BlockSpec((None, d, hdim), lambda i, be, nu, tk: (be[i], 0, 0)),
                  pl.BlockSpec((None, d, hdim), lambda i, be, nu, tk: (be[i], 0, 0)),
                  pl.BlockSpec((None, hdim, d), lambda i, be, nu, tk: (be[i], 0, 0)),
                  pl.BlockSpec((MOE_BLOCK, 1), lambda i, be, nu, tk: (i, 0))],
        out_specs=pl.BlockSpec((MOE_BLOCK, d), lambda i, be, nu, tk: (i, 0)),
        scratch_shapes=[pltpu.VMEM((2, MOE_BLOCK, d), F32), pltpu.SemaphoreType.DMA((2,))],
    )
    return pl.pallas_call(
        _moe_kernel,
        grid_spec=grid_spec,
        out_shape=jax.ShapeDtypeStruct((p, d), F32),
        compiler_params=_cparams(("arbitrary",)),
        name="moe_experts",
    )(blk_e, n_used, buf_tok, f, w1, w3, w2, buf_w.reshape(p, 1))


def _combine_kernel(pos_ref, h_ref, ys_hbm, nw_ref, o_ref, yg_ref, sem):
    i = pl.program_id(0)
    n_steps = pl.num_programs(0)
    tm = h_ref.shape[0]
    rows = yg_ref.shape[1]

    def start(step, slot):
        def body(r, _):
            _row_copy(ys_hbm, yg_ref, sem, pos_ref[step * rows + r], slot, r).start()
            return 0
        lax.fori_loop(0, rows, body, 0)

    def wait(slot):
        def body(r, _):
            _row_copy(ys_hbm, yg_ref, sem, 0, slot, r).wait()
            return 0
        lax.fori_loop(0, rows, body, 0)

    @pl.when(i == 0)
    def _():
        start(0, 0)

    @pl.when(i + 1 < n_steps)
    def _():
        start(i + 1, (i + 1) % 2)

    slot = i % 2
    wait(slot)
    h = h_ref[...]
    for k in range(TOP_K_INNER):
        h = h + yg_ref[slot, k * tm:(k + 1) * tm, :]
    y = h * lax.rsqrt(jnp.mean(h * h, axis=-1, keepdims=True) + RMS_EPS)
    o_ref[...] = y * nw_ref[...]


def _combine(h, ys, pos, norm_w):
    n, d = h.shape
    tm = _tile(n, 128)
    grid_spec = pltpu.PrefetchScalarGridSpec(
        num_scalar_prefetch=1,
        grid=(n // tm,),
        in_specs=[pl.BlockSpec((tm, d), lambda i, ps: (i, 0)),
                  pl.BlockSpec(memory_space=pl.ANY),
                  pl.BlockSpec((1, d), lambda i, ps: (0, 0))],
        out_specs=pl.BlockSpec((tm, d), lambda i, ps: (i, 0)),
        scratch_shapes=[pltpu.VMEM((2, TOP_K_INNER * tm, d), F32), pltpu.SemaphoreType.DMA((2,))],
    )
    return pl.pallas_call(
        _combine_kernel,
        grid_spec=grid_spec,
        out_shape=jax.ShapeDtypeStruct((n, d), F32),
        compiler_params=_cparams(("arbitrary",)),
        name="moe_combine_final_norm",
    )(pos, h, ys, norm_w.reshape(1, d).astype(F32))


def _routing(logits, n_tok, tm_combine):
    ng, epg, kk = N_EXPERT_GROUPS, EXPERTS_PER_GROUP, TOP_K_INNER
    n_exp = ng * epg
    p_group = jax.nn.softmax(logits[:, :ng], axis=-1)
    g_prob, g_sel = lax.top_k(p_group, 1)
    logits_e = logits[:, ng:ng + n_exp].reshape(n_tok, ng, epg)
    logits_in = jnp.take_along_axis(logits_e, g_sel[:, :, None], axis=1)[:, 0]
    top_v, top_i = lax.top_k(logits_in, kk)
    gate = jax.nn.softmax(top_v, axis=-1) * g_prob
    e_idx = (g_sel * epg + top_i).astype(jnp.int32)

    m = n_tok * kk
    flat_e = e_idx.reshape(m)
    flat_tok = jnp.repeat(jnp.arange(n_tok, dtype=jnp.int32), kk)
    flat_w = gate.reshape(m)
    order = jnp.argsort(flat_e, stable=True)
    se, stok, sw = flat_e[order], flat_tok[order], flat_w[order]
    counts = jnp.bincount(flat_e, length=n_exp)
    padded = ((counts + MOE_BLOCK - 1) // MOE_BLOCK) * MOE_BLOCK
    pad_end = jnp.cumsum(padded)
    pad_start = pad_end - padded
    grp_start = jnp.cumsum(counts) - counts
    dest = (pad_start[se] + jnp.arange(m, dtype=jnp.int32) - grp_start[se]).astype(jnp.int32)
    n_blocks = -(-m // MOE_BLOCK) + n_exp
    p = n_blocks * MOE_BLOCK
    buf_tok = jnp.zeros((p,), jnp.int32).at[dest].set(stok)
    buf_w = jnp.zeros((p,), F32).at[dest].set(sw)
    blk_e = jnp.minimum(jnp.searchsorted(pad_end, jnp.arange(n_blocks) * MOE_BLOCK, side='right'),
                        n_exp - 1).astype(jnp.int32)
    n_used = (pad_end[-1] // MOE_BLOCK).astype(jnp.int32).reshape(1)
    pos_flat = jnp.zeros((m,), jnp.int32).at[order].set(dest)
    pos = pos_flat.reshape(n_tok // tm_combine, tm_combine, kk).transpose(0, 2, 1).reshape(m)
    return blk_e, n_used, buf_tok, buf_w, pos


def kernel(x, meta_tokens, norm_mix_w, w_in, s5_a_re, s5_a_im, s5_log_dt, s5_b_re, s5_b_im, s5_c_re, s5_c_im, s5_d, s5_glu_w, s5_glu_b, hg_lb_gamma, hg_norm_w, w_branch_a, w_branch_b, w_out, norm_ffn_w, router_group_w, router_group_b, router_expert_w, router_expert_b, expert_w1, expert_w3, expert_w2, norm_final_w):
    n_batch, seq, d_model = x.shape
    depth = w_in.shape[0]
    n_tok = n_batch * seq
    d_s5 = s5_d.shape[-1]
    d_hg = hg_norm_w.shape[-1]
    T = S5_CHUNK
    assert meta_tokens.shape[0] == N_META == T and seq % GLA_CHUNK == 0
    jb = d_s5 // LANES
    sig = jax.nn.sigmoid
    o_u, o_q, o_i, o_f, o_g, o_ga = 0, d_s5, d_s5 + d_hg, d_s5 + 2 * d_hg, d_s5 + 4 * d_hg, d_s5 + 5 * d_hg

    h = x.reshape(n_tok, d_model)
    h_meta = meta_tokens.astype(x.dtype)
    for l in range(depth):
        assert l == 0, "the meta-token shortcut is only valid for a single layer"
        a = _rmsnorm(h, norm_mix_w[l])
        a_m = _rmsnorm(h_meta, norm_mix_w[l])
        w = w_in[l]
        proj = functools.partial(_matmul, [(a, w)])
        proj_m = functools.partial(_matmul, [(a_m, w)], out_dtype=F32)

        u_blk = proj(n_cols=d_s5, col_off=o_u, lane_blocked=True, name="proj_u")
        q = proj(n_cols=d_hg, col_off=o_q, epilogue=jax.nn.silu, name="proj_q")
        v = proj(n_cols=d_hg, col_off=o_i, name="proj_i")
        z_fb = proj(n_cols=2 * d_hg, col_off=o_f, out_dtype=F32, name="proj_f")
        g_out = proj(n_cols=d_hg, col_off=o_g, epilogue=jax.nn.silu, name="proj_g")
        gates = proj(n_cols=2 * d_model, col_off=o_ga, epilogue=sig, name="proj_gates")
        u_m = proj_m(n_cols=d_s5, col_off=o_u, name="proj_u_meta")
        v_m = proj_m(n_cols=d_hg, col_off=o_i, out_dtype=BF16, name="proj_i_meta")
        zf_m = proj_m(n_cols=d_hg, col_off=o_f, name="proj_f_meta")

        e_mat, m_mat, f_mat, lam_t = _s5_matrices(
            s5_a_re[l].astype(F32), s5_a_im[l].astype(F32), s5_log_dt[l].astype(F32), s5_b_re[l].astype(F32),
            s5_b_im[l].astype(F32), s5_c_re[l].astype(F32), s5_c_im[l].astype(F32), s5_d[l].astype(F32))
        sw = e_mat.shape[-1]
        u_ch = u_blk.reshape(jb, n_tok // T, T * LANES)
        um = u_m.reshape(T, jb, LANES).transpose(1, 0, 2).reshape(jb, 1, T * LANES)
        um = jnp.pad(um, ((0, 0), (0, 7), (0, 0))).astype(BF16)
        x0 = _matmul([(um, e_mat)], n_cols=sw, out_dtype=F32, name="s5_state_meta")[:, 0:1, :]
        x0 = jnp.concatenate([x0[..., :sw // 2], jnp.zeros_like(x0[..., sw // 2:])], axis=-1)
        s_all = _matmul([(u_ch, e_mat)], n_cols=sw, out_dtype=F32, name="s5_chunk_state")
        x_prev = _s5_scan(s_all, lam_t, x0, n_batch)
        y_blk = _matmul([(u_ch, m_mat), (x_prev, f_mat)], n_cols=T * LANES, name="s5_chunk_out")
        s5_out = _glu(y_blk.reshape(jb, n_tok, LANES), s5_glu_w[l], s5_glu_b[l])

        lb = jnp.cumsum(jax.nn.softmax(hg_lb_gamma.astype(F32), axis=1), axis=1)[:, l]
        hg_out = _gla(q, v, z_fb[:, :d_hg], z_fb[:, d_hg:], g_out, v_m, zf_m, lb, hg_norm_w[l], n_batch)

        mrg = _merge(s5_out, w_branch_a[l], hg_out, w_branch_b[l], gates[:, :d_model], gates[:, d_model:])
        h = _matmul([(mrg, w_out[l])], n_cols=d_model, extras=(h,), epilogue=lambda acc, r: acc + r,
                    out_dtype=F32, name="out_proj")

        ng, n_exp = router_group_w.shape[-1], router_expert_w.shape[-1]
        rw = jnp.concatenate([router_group_w[l], router_expert_w[l],
                              jnp.zeros((d_model, ROUTER_COLS - ng - n_exp), F32)], axis=1).astype(F32)
        rb = jnp.concatenate([router_group_b[l], router_expert_b[l],
                              jnp.zeros((ROUTER_COLS - ng - n_exp,), F32)]).reshape(1, ROUTER_COLS).astype(F32)
        f, logits = _router(h, norm_ffn_w[l], rw, rb)
        tm_c = _tile(n_tok, 128)
        blk_e, n_used, buf_tok, buf_w, pos = _routing(logits, n_tok, tm_c)
        ys = _moe(f, expert_w1[l].astype(BF16), expert_w3[l].astype(BF16), expert_w2[l].astype(BF16),
                  blk_e, n_used, buf_tok, buf_w)
    out = _combine(h, ys, pos, norm_final_w)
    return out.reshape(n_batch, seq, d_model)
```

```python
import functools
import math

import jax
import jax.numpy as jnp
from jax import lax
from jax.experimental import pallas as pl
from jax.experimental.pallas import tpu as pltpu

F32 = jnp.float32
BF16 = jnp.bfloat16

N_META = 16
S5_GROUP = 16
S5_STATE = 64
S5_CHUNK = 16
HG_EXPAND = 128
GLA_CHUNK = 64
GLA_TILE = 256
N_EXPERT_GROUPS = 8
EXPERTS_PER_GROUP = 8
TOP_K_INNER = 2
MOE_BLOCK = 256
RMS_EPS = 1e-6
LANES = 128
ROUTER_COLS = 128
VMEM_LIMIT = 56 * 1024 * 1024


def _cparams(sem):
    return pltpu.CompilerParams(dimension_semantics=sem, vmem_limit_bytes=VMEM_LIMIT)


def _tile(n, pref):
    t = min(n, pref)
    assert n % t == 0, (n, pref)
    return t


def _rmsnorm_kernel(x_ref, w_ref, o_ref):
    x = x_ref[...]
    y = x * lax.rsqrt(jnp.mean(x * x, axis=-1, keepdims=True) + RMS_EPS)
    o_ref[...] = (y * w_ref[...]).astype(o_ref.dtype)


def _rmsnorm(x, w, out_dtype=BF16):
    n, d = x.shape
    tm = _tile(n, 256)
    return pl.pallas_call(
        _rmsnorm_kernel,
        grid=(n // tm,),
        in_specs=[pl.BlockSpec((tm, d), lambda i: (i, 0)), pl.BlockSpec((1, d), lambda i: (0, 0))],
        out_specs=pl.BlockSpec((tm, d), lambda i: (i, 0)),
        out_shape=jax.ShapeDtypeStruct((n, d), out_dtype),
        compiler_params=_cparams(("parallel",)),
        name="rmsnorm",
    )(x, w.reshape(1, d).astype(F32))


def _mm_kernel(*refs, n_pairs, n_extra, epilogue, lane_blocked):
    a_refs = refs[0:2 * n_pairs:2]
    w_refs = refs[1:2 * n_pairs:2]
    extras = refs[2 * n_pairs:2 * n_pairs + n_extra]
    o_ref = refs[2 * n_pairs + n_extra]
    acc = None
    for a_ref, w_ref in zip(a_refs, w_refs):
        d = jnp.dot(a_ref[...], w_ref[...], preferred_element_type=F32)
        acc = d if acc is None else acc + d
    res = epilogue(acc, *[e[...] for e in extras]).astype(o_ref.dtype)
    if lane_blocked:
        for q in range(o_ref.shape[0]):
            o_ref[q] = res[:, q * LANES:(q + 1) * LANES]
    else:
        o_ref[...] = res


def _matmul(pairs, *, n_cols, col_off=0, extras=(), epilogue=None, out_dtype=BF16, lane_blocked=False,
            tm=1024, tn=1024, name="matmul"):
    if epilogue is None:
        epilogue = lambda acc: acc
    a0 = pairs[0][0]
    batched = a0.ndim == 3
    g = a0.shape[0] if batched else 1
    m = a0.shape[-2]
    tm = _tile(m, tm)
    tn = _tile(math.gcd(n_cols, col_off) if col_off else n_cols, tn)
    cb = col_off // tn
    lead = (None,) if batched else ()

    def bidx(b):
        return (b,) if batched else ()

    in_specs, args = [], []
    for a, w in pairs:
        kdim = a.shape[-1]
        assert a.dtype == BF16 and w.dtype == BF16 and w.shape[-2] == kdim and a.shape[-2] == m
        in_specs.append(pl.BlockSpec(lead + (tm, kdim), lambda b, i, j: bidx(b) + (i, 0)))
        in_specs.append(pl.BlockSpec(lead + (kdim, tn), lambda b, i, j: bidx(b) + (0, j + cb)))
        args += [a, w]
    for e in extras:
        if e.shape[-2] == 1:
            in_specs.append(pl.BlockSpec((1, tn), lambda b, i, j: (0, j)))
        else:
            in_specs.append(pl.BlockSpec((tm, tn), lambda b, i, j: (i, j)))
        args.append(e)
    if lane_blocked:
        assert not batched
        out_shape = jax.ShapeDtypeStruct((n_cols // LANES, m, LANES), out_dtype)
        out_spec = pl.BlockSpec((tn // LANES, tm, LANES), lambda b, i, j: (j, i, 0))
    else:
        out_shape = jax.ShapeDtypeStruct(((g,) if batched else ()) + (m, n_cols), out_dtype)
        out_spec = pl.BlockSpec(lead + (tm, tn), lambda b, i, j: bidx(b) + (i, j))
    return pl.pallas_call(
        functools.partial(_mm_kernel, n_pairs=len(pairs), n_extra=len(extras), epilogue=epilogue,
                          lane_blocked=lane_blocked),
        grid=(g, m // tm, n_cols // tn),
        in_specs=in_specs,
        out_specs=out_spec,
        out_shape=out_shape,
        compiler_params=_cparams(("parallel", "parallel", "parallel")),
        name=name,
    )(*args)


def _s5_compact_operators(a_re, a_im, log_dt, b_re, b_im, c_re, c_im, d_skip):
    T = S5_CHUNK
    hp = lax.Precision.HIGHEST
    G, P = a_re.shape[1:]
    H = b_re.shape[-1]
    GB = LANES // H
    J = G // GB
    dt = jnp.exp(log_dt)[..., None]
    lre, lang = a_re * dt, a_im * dt
    mag = jnp.exp(lre)
    ab_re, ab_im = mag * jnp.cos(lang), mag * jnp.sin(lang)
    den = a_re * a_re + a_im * a_im
    nr, ni = ab_re - 1.0, ab_im
    z_re = ((nr * a_re + ni * a_im) / den)[..., None]
    z_im = ((ni * a_re - nr * a_im) / den)[..., None]
    bb_re = z_re * b_re - z_im * b_im
    bb_im = z_re * b_im + z_im * b_re
    n = jnp.arange(T + 1, dtype=F32)
    pmag = jnp.exp(lre[..., None] * n)
    pw_re = pmag * jnp.cos(lang[..., None] * n)
    pw_im = pmag * jnp.sin(lang[..., None] * n)
    cl_re = c_re[..., None] * pw_re[:, :, None] - c_im[..., None] * pw_im[:, :, None]
    cl_im = c_re[..., None] * pw_im[:, :, None] + c_im[..., None] * pw_re[:, :, None]
    kk = (jnp.einsum('dgqpn,dgph->dgnqh', cl_re, bb_re, precision=hp)
          - jnp.einsum('dgqpn,dgph->dgnqh', cl_im, bb_im, precision=hp))
    s_idx = jnp.arange(T)[:, None]
    t_idx = jnp.arange(T)[None, :]
    kf = jnp.where((t_idx >= s_idx)[None, :, :, None, None], kk[0][:, jnp.clip(t_idx - s_idx, 0, T)], 0.0)
    kb = jnp.where((s_idx >= t_idx)[None, :, :, None, None], kk[1][:, jnp.clip(s_idx - t_idx, 0, T)], 0.0)
    eye_t = jnp.eye(T, dtype=F32)
    eye_h = jnp.eye(H, dtype=F32)
    dd = d_skip.reshape(G, H)
    kt = kf + kb + eye_t[None, :, :, None, None] * (eye_h[None] * dd[:, :, None])[:, None, None]
    mc = kt.reshape(J, GB, T, T, H, H).transpose(0, 2, 1, 5, 3, 4).reshape(J, T * LANES, T * H)

    ar = jnp.arange(T)
    e_parts = []
    for d, idx in ((0, T - 1 - ar), (1, ar)):
        pr, pi = pw_re[d][:, :, idx], pw_im[d][:, :, idx]
        e_re = pr[..., None] * bb_re[d][:, :, None, :] - pi[..., None] * bb_im[d][:, :, None, :]
        e_im = pr[..., None] * bb_im[d][:, :, None, :] + pi[..., None] * bb_re[d][:, :, None, :]
        e_parts += [e_re, e_im]
    ee = jnp.stack(e_parts, 0).reshape(4, J, GB, P, T, H)
    ec = ee.transpose(1, 4, 2, 5, 0, 3).reshape(J, T * LANES, 4 * P)

    f_parts = []
    for d, idx in ((0, ar + 1), (1, T - ar)):
        f_parts += [cl_re[d][..., idx], -cl_im[d][..., idx]]
    ff = jnp.stack(f_parts, 0).reshape(4, J, GB, H, P, T)
    fc = ff.transpose(1, 0, 2, 4, 5, 3).reshape(J, 4 * GB * P, T * H)

    lam_t = jnp.stack([pw_re[0][..., T], pw_im[0][..., T], pw_re[1][..., T], pw_im[1][..., T]], 0)
    lam_t = lam_t.reshape(4, J, GB * P).transpose(1, 0, 2).reshape(J, 1, 4 * GB * P)
    return ec.astype(BF16), mc.astype(BF16), fc.astype(BF16), lam_t


def _replicator(n_outer, n_inner, reps):
    rows = jnp.arange(n_outer * n_inner)
    cols = jnp.arange(n_outer * reps * n_inner)
    same_o = (rows[:, None] // n_inner) == (cols[None, :] // (reps * n_inner))
    same_i = (rows[:, None] % n_inner) == (cols[None, :] % n_inner)
    return (same_o & same_i).astype(BF16)


def _expand_kernel(c_ref, r_ref, o_ref, *, row_group, col_group, n_groups):
    r, tn = o_ref.shape
    rep = jnp.dot(c_ref[...], r_ref[...], preferred_element_type=F32)
    rows = lax.broadcasted_iota(jnp.int32, (r, 1), 0)
    cols = lax.broadcasted_iota(jnp.int32, (1, tn), 1) + pl.program_id(1) * tn
    row_g = (rows >> int(math.log2(row_group))) & (n_groups - 1)
    col_g = (cols >> int(math.log2(col_group))) & (n_groups - 1)
    o_ref[...] = jnp.where(row_g == col_g, rep, 0.0).astype(o_ref.dtype)


def _expand(compact, rmat, row_group, col_group, n_groups, name):
    j, r, c = compact.shape
    n_cols = rmat.shape[1]
    tn = _tile(n_cols, 1024)
    return pl.pallas_call(
        functools.partial(_expand_kernel, row_group=row_group, col_group=col_group, n_groups=n_groups),
        grid=(j, n_cols // tn),
        in_specs=[pl.BlockSpec((None, r, c), lambda i, k: (i, 0, 0)), pl.BlockSpec((c, tn), lambda i, k: (0, k))],
        out_specs=pl.BlockSpec((None, r, tn), lambda i, k: (i, 0, k)),
        out_shape=jax.ShapeDtypeStruct((j, r, n_cols), BF16),
        compiler_params=_cparams(("parallel", "parallel")),
        name=name,
    )(compact, rmat)


def _s5_scan_kernel(s_ref, lam_ref, x0_ref, o_ref, xp_ref, *, n_batch, n_chunks):
    hw = s_ref.shape[-1] // 4
    lam = lam_ref[...]
    lfr, lfi, lbr, lbi = (lam[:, q * hw:(q + 1) * hw] for q in range(4))
    x0 = x0_ref[...]
    zeros = jnp.zeros((1, hw), F32)
    init = tuple((x0[:, 0:hw], x0[:, hw:2 * hw], zeros, zeros) for _ in range(n_batch))

    def step(c, carry):
        out = []
        for b, (xfr, xfi, xbr, xbi) in enumerate(carry):
            row_f = pl.ds(b * n_chunks + c, 1)
            row_b = pl.ds(b * n_chunks + n_chunks - 1 - c, 1)
            xp_ref[row_f, 0:hw] = xfr
            xp_ref[row_f, hw:2 * hw] = xfi
            xp_ref[row_b, 2 * hw:3 * hw] = xbr
            xp_ref[row_b, 3 * hw:4 * hw] = xbi
            sf = s_ref[row_f, 0:2 * hw]
            sb = s_ref[row_b, 2 * hw:4 * hw]
            out.append((lfr * xfr - lfi * xfi + sf[:, 0:hw],
                        lfr * xfi + lfi * xfr + sf[:, hw:2 * hw],
                        lbr * xbr - lbi * xbi + sb[:, 0:hw],
                        lbr * xbi + lbi * xbr + sb[:, hw:2 * hw]))
        return tuple(out)

    lax.fori_loop(0, n_chunks, step, init)
    o_ref[...] = xp_ref[...].astype(o_ref.dtype)


def _s5_scan(s_all, lam_t, x0, n_batch):
    j, r, w = s_all.shape
    n_chunks = r // n_batch
    return pl.pallas_call(
        functools.partial(_s5_scan_kernel, n_batch=n_batch, n_chunks=n_chunks),
        grid=(j,),
        in_specs=[pl.BlockSpec((None, r, w), lambda i: (i, 0, 0)),
                  pl.BlockSpec((None, 1, w), lambda i: (i, 0, 0)),
                  pl.BlockSpec((None, 1, w), lambda i: (i, 0, 0))],
        out_specs=pl.BlockSpec((None, r, w), lambda i: (i, 0, 0)),
        out_shape=jax.ShapeDtypeStruct((j, r, w), BF16),
        scratch_shapes=[pltpu.VMEM((r, w), F32)],
        compiler_params=_cparams(("parallel",)),
        name="s5_scan",
    )(s_all, lam_t, x0)


def _glu_kernel(y_ref, w_ref, b_ref, o_ref, z_ref):
    jn = pl.program_id(1)
    tn = o_ref.shape[-1]

    @pl.when(jn == 0)
    def _():
        for q in range(y_ref.shape[0]):
            z_ref[:, q * LANES:(q + 1) * LANES] = jax.nn.gelu(y_ref[q].astype(F32)).astype(z_ref.dtype)

    gate = jnp.dot(z_ref[...], w_ref[...], preferred_element_type=F32) + b_ref[...]
    zc = z_ref[:, pl.ds(pl.multiple_of(jn * tn, tn), tn)].astype(F32)
    o_ref[...] = (zc * jax.nn.sigmoid(gate)).astype(o_ref.dtype)


def _glu(y_blk, glu_w, glu_b):
    jb, n, _ = y_blk.shape
    d = jb * LANES
    tm, tn = _tile(n, 1024), _tile(d, 1024)
    return pl.pallas_call(
        _glu_kernel,
        grid=(n // tm, d // tn),
        in_specs=[pl.BlockSpec((jb, tm, LANES), lambda i, j: (0, i, 0)),
                  pl.BlockSpec((d, tn), lambda i, j: (0, j)),
                  pl.BlockSpec((1, tn), lambda i, j: (0, j))],
        out_specs=pl.BlockSpec((tm, tn), lambda i, j: (i, j)),
        out_shape=jax.ShapeDtypeStruct((n, d), BF16),
        scratch_shapes=[pltpu.VMEM((tm, d), BF16)],
        compiler_params=_cparams(("parallel", "arbitrary")),
        name="s5_glu",
    )(y_blk, glu_w, glu_b.reshape(1, d).astype(F32))


def _nt(a, b):
    return lax.dot_general(a, b, (((1,), (1,)), ((), ())), preferred_element_type=F32)


def _split3(x):
    hi = x.astype(BF16)
    r1 = x - hi.astype(F32)
    mid = r1.astype(BF16)
    lo = (r1 - mid.astype(F32)).astype(BF16)
    return jnp.concatenate([hi, mid, lo], axis=0)


def _gates(z, lbd):
    f = lbd + (1.0 - lbd) * jax.nn.sigmoid(z)
    return 1.0 - f, jnp.log(f)


def _gla_kernel(q_ref, v_ref, zf_ref, zb_ref, go_ref, vm_ref, zm_ref, lb_ref, nw_ref, o_ref,
                acc_ref, vt_ref, k_ref, g3_ref, b_ref, qt_ref, kt_ref, qe_ref, kh_ref, u_ref, dl_ref, st_ref):
    L, K = q_ref.shape
    C = GLA_CHUNK
    TR = GLA_TILE
    cpt = TR // C
    n_chunks = L // C
    n_tiles = L // TR
    lb = lb_ref[...]
    shift_c, shift_k = int(math.log2(C)), int(math.log2(K))
    row = lax.broadcasted_iota(jnp.int32, (TR, TR), 0)
    col = lax.broadcasted_iota(jnp.int32, (TR, TR), 1)
    same_chunk = (row >> shift_c) == (col >> shift_c)
    own_block = ((lax.broadcasted_iota(jnp.int32, (TR, cpt * K), 0) >> shift_c)
                 == (lax.broadcasted_iota(jnp.int32, (TR, cpt * K), 1) >> shift_k))

    def crow(c):
        return pl.ds(pl.multiple_of(c * C, C), C)

    def trow(t):
        return pl.ds(pl.multiple_of(t * TR, TR), TR)

    def block_diag(x):
        return jnp.where(own_block, jnp.concatenate([x] * cpt, axis=1), jnp.zeros((), x.dtype))

    def transpose_v(t, _):
        vt_ref[t] = v_ref[trow(t), :].astype(F32).T.astype(BF16)
        return 0

    lax.fori_loop(0, n_tiles, transpose_v, 0, unroll=2)

    m = zm_ref.shape[0]
    km, gm = _gates(zm_ref[...], lb[0:1, :])
    mrow = lax.broadcasted_iota(jnp.int32, (m, m), 0)
    mcol = lax.broadcasted_iota(jnp.int32, (m, m), 1)
    bm = jnp.dot(jnp.concatenate([(mrow >= mcol).astype(BF16)] * 3, axis=1), _split3(gm),
                 preferred_element_type=F32)
    khm = (km * jnp.exp(bm[m - 1:m, :] - bm)).astype(BF16)
    st0 = jnp.dot(vm_ref[...].astype(F32).T.astype(BF16), khm, preferred_element_type=F32)

    for d, z_ref in enumerate((zf_ref, zb_ref)):
        lbd = lb[d:d + 1, :]
        causal = (row >= col) if d == 0 else (col >= row)
        keep = same_chunk & causal
        cum = jnp.concatenate([causal[:C, :C].astype(BF16)] * 3, axis=1)
        ref_row = C // 2 - 1 if d == 0 else C // 2
        edge_row = C - 1 if d == 0 else 0

        def gate_stage(t, _):
            k, g = _gates(z_ref[trow(t), :], lbd)
            k_ref[trow(t), :] = k
            hi = g.astype(BF16)
            r1 = g - hi.astype(F32)
            mid = r1.astype(BF16)
            lo = (r1 - mid.astype(F32)).astype(BF16)
            for part, val in enumerate((hi, mid, lo)):
                for j in range(cpt):
                    g3_ref[t, part * C:(part + 1) * C, j * K:(j + 1) * K] = val[j * C:(j + 1) * C, :]
            return 0

        lax.fori_loop(0, n_tiles, gate_stage, 0, unroll=2)

        def cumsum_stage(t, _):
            bw = jnp.dot(cum, g3_ref[t], preferred_element_type=F32)
            for j in range(cpt):
                b_ref[crow(t * cpt + j), :] = bw[:, j * K:(j + 1) * K]
            return 0

        lax.fori_loop(0, n_tiles, cumsum_stage, 0, unroll=4)

        def scale_stage(c, _):
            rows = crow(c)
            b = b_ref[rows, :]
            r = b[ref_row:ref_row + 1, :]
            e = b[edge_row:edge_row + 1, :]
            qt = q_ref[rows, :].astype(F32) * jnp.exp(b - r)
            kt = k_ref[rows, :] * jnp.exp(r - b)
            qt_ref[rows, :] = qt.astype(BF16)
            kt_ref[rows, :] = kt.astype(BF16)
            qe_ref[rows, :] = (qt * jnp.exp(r)).astype(BF16)
            kh_ref[rows, :] = (kt * jnp.exp(e - r)).astype(BF16)
            dl_ref[pl.ds(c, 1), :] = jnp.exp(e)
            return 0

        lax.fori_loop(0, n_chunks, scale_stage, 0, unroll=2)

        def intra_stage(t, _):
            rows = trow(t)
            s = jnp.where(keep, _nt(qt_ref[rows, :], kt_ref[rows, :]), 0.0).astype(BF16)
            acc_ref[d, rows, :] = jnp.dot(s, v_ref[rows, :], preferred_element_type=F32)
            u_all = jnp.dot(vt_ref[t], block_diag(kh_ref[rows, :]), preferred_element_type=F32)
            for j in range(cpt):
                u_ref[t * cpt + j] = u_all[:, j * K:(j + 1) * K]
            return 0

        lax.fori_loop(0, n_tiles, intra_stage, 0, unroll=2)

        def state_stage(i, st):
            c = i if d == 0 else n_chunks - 1 - i
            st_ref[c] = st.astype(BF16)
            return st * dl_ref[pl.ds(c, 1), :] + u_ref[c]

        lax.fori_loop(0, n_chunks, state_stage, st0 if d == 0 else jnp.zeros((K, K), F32), unroll=4)

        def inter_stage(t, _):
            rows = trow(t)
            st_cat = jnp.concatenate([st_ref[t * cpt + j] for j in range(cpt)], axis=1)
            acc_ref[d, rows, :] += _nt(block_diag(qe_ref[rows, :]), st_cat)
            return 0

        lax.fori_loop(0, n_tiles, inter_stage, 0, unroll=2)

    o = acc_ref[0] + acc_ref[1]
    o = o * lax.rsqrt(jnp.mean(o * o, axis=-1, keepdims=True) + RMS_EPS)
    o_ref[...] = (o * nw_ref[...] * go_ref[...].astype(F32)).astype(o_ref.dtype)


def _gla(q, v, z_fb, go, v_meta, zf_meta, lb, norm_w, n_batch):
    n, d = q.shape
    L = n // n_batch
    heads = d // HG_EXPAND
    m = v_meta.shape[0]
    C = GLA_CHUNK
    n_chunks = L // C
    hk = HG_EXPAND
    tok = pl.BlockSpec((L, hk), lambda b, h: (b, h))
    tok_bwd = pl.BlockSpec((L, hk), lambda b, h: (b, h + heads))
    return pl.pallas_call(
        _gla_kernel,
        grid=(n_batch, heads),
        in_specs=[tok, tok, tok, tok_bwd, tok,
                  pl.BlockSpec((m, hk), lambda b, h: (0, h)),
                  pl.BlockSpec((m, hk), lambda b, h: (0, h)),
                  pl.BlockSpec((2, hk), lambda b, h: (0, h)),
                  pl.BlockSpec((1, hk), lambda b, h: (0, h))],
        out_specs=tok,
        out_shape=jax.ShapeDtypeStruct((n, d), BF16),
        scratch_shapes=[pltpu.VMEM((2, L, hk), F32),
                        pltpu.VMEM((L // GLA_TILE, hk, GLA_TILE), BF16),
                        pltpu.VMEM((L, hk), F32),
                        pltpu.VMEM((L // GLA_TILE, 3 * C, (GLA_TILE // C) * hk), BF16),
                        pltpu.VMEM((L, hk), F32),
                        pltpu.VMEM((L, hk), BF16),
                        pltpu.VMEM((L, hk), BF16),
                        pltpu.VMEM((L, hk), BF16),
                        pltpu.VMEM((L, hk), BF16),
                        pltpu.VMEM((n_chunks, hk, hk), F32),
                        pltpu.VMEM((n_chunks, hk), F32),
                        pltpu.VMEM((n_chunks, hk, hk), BF16)],
        compiler_params=_cparams(("parallel", "parallel")),
        name="hgrn2_gla",
    )(q, v, z_fb, z_fb, go, v_meta, zf_meta, lb, norm_w.reshape(1, d).astype(F32))


def _merge_kernel(a1_ref, w1_ref, a2_ref, w2_ref, g1_ref, g2_ref, o_ref):
    y1 = jnp.dot(a1_ref[...], w1_ref[...], preferred_element_type=F32)
    y2 = jnp.dot(a2_ref[...], w2_ref[...], preferred_element_type=F32)
    o_ref[...] = (g1_ref[...].astype(F32) * y1 + g2_ref[...].astype(F32) * y2).astype(o_ref.dtype)


def _merge(a1, w1, a2, w2, gates):
    n, k1 = a1.shape
    k2 = a2.shape[1]
    d = w1.shape[1]
    tm, tn = _tile(n, 1024), _tile(d, 1024)
    nj = d // tn
    o_spec = pl.BlockSpec((tm, tn), lambda i, j: (i, j))
    return pl.pallas_call(
        _merge_kernel,
        grid=(n // tm, nj),
        in_specs=[pl.BlockSpec((tm, k1), lambda i, j: (i, 0)), pl.BlockSpec((k1, tn), lambda i, j: (0, j)),
                  pl.BlockSpec((tm, k2), lambda i, j: (i, 0)), pl.BlockSpec((k2, tn), lambda i, j: (0, j)),
                  o_spec, pl.BlockSpec((tm, tn), lambda i, j: (i, j + nj))],
        out_specs=o_spec,
        out_shape=jax.ShapeDtypeStruct((n, d), BF16),
        compiler_params=_cparams(("parallel", "parallel")),
        name="branch_merge",
    )(a1, w1, a2, w2, gates, gates)


def _router_kernel(h_ref, nw_ref, rw_ref, rb_ref, f_ref, lg_ref):
    x = h_ref[...]
    f = x * lax.rsqrt(jnp.mean(x * x, axis=-1, keepdims=True) + RMS_EPS) * nw_ref[...]
    f_ref[...] = f
    lg_ref[...] = jnp.dot(f, rw_ref[...], precision=lax.Precision.HIGHEST,
                          preferred_element_type=F32) + rb_ref[...]


def _router(h, norm_w, rw, rb):
    n, d = h.shape
    tm = _tile(n, 256)
    return pl.pallas_call(
        _router_kernel,
        grid=(n // tm,),
        in_specs=[pl.BlockSpec((tm, d), lambda i: (i, 0)), pl.BlockSpec((1, d), lambda i: (0, 0)),
                  pl.BlockSpec((d, ROUTER_COLS), lambda i: (0, 0)),
                  pl.BlockSpec((1, ROUTER_COLS), lambda i: (0, 0))],
        out_specs=[pl.BlockSpec((tm, d), lambda i: (i, 0)), pl.BlockSpec((tm, ROUTER_COLS), lambda i: (i, 0))],
        out_shape=[jax.ShapeDtypeStruct((n, d), F32), jax.ShapeDtypeStruct((n, ROUTER_COLS), F32)],
        compiler_params=_cparams(("parallel",)),
        name="ffn_norm_router",
    )(h, norm_w.reshape(1, d).astype(F32), rw, rb)


def _row_copy(src_hbm, dst_vmem, sem, src_row, slot, dst_row):
    return pltpu.make_async_copy(src_hbm.at[pl.ds(src_row, 1)], dst_vmem.at[slot, pl.ds(dst_row, 1)], sem.at[slot])


def _moe_kernel(blk_e_ref, n_used_ref, tok_ref, f_hbm, w1_ref, w3_ref, w2_ref, gw_ref, o_ref, xs_ref, sem):
    del blk_e_ref
    i = pl.program_id(0)
    n_used = n_used_ref[0]
    rows = xs_ref.shape[1]

    def start(blk, slot):
        def body(r, _):
            _row_copy(f_hbm, xs_ref, sem, tok_ref[blk * rows + r], slot, r).start()
            return 0
        lax.fori_loop(0, rows, body, 0, unroll=8)

    def wait(slot):
        def body(r, _):
            _row_copy(f_hbm, xs_ref, sem, 0, slot, r).wait()
            return 0
        lax.fori_loop(0, rows, body, 0, unroll=8)

    @pl.when(jnp.logical_and(i == 0, n_used > 0))
    def _():
        start(0, 0)

    @pl.when(i + 1 < n_used)
    def _():
        start(i + 1, (i + 1) % 2)

    @pl.when(i < n_used)
    def _():
        slot = i % 2
        wait(slot)
        x = xs_ref[slot].astype(BF16)
        h1 = jnp.dot(x, w1_ref[...].astype(BF16), preferred_element_type=F32)
        h3 = jnp.dot(x, w3_ref[...].astype(BF16), preferred_element_type=F32)
        hb = (jax.nn.silu(h1) * h3).astype(BF16)
        y = jnp.dot(hb, w2_ref[...].astype(BF16), preferred_element_type=F32)
        o_ref[...] = (y * gw_ref[...]).astype(o_ref.dtype)

    @pl.when(i >= n_used)
    def _():
        o_ref[...] = jnp.zeros_like(o_ref)


def _moe(f, w1, w3, w2, blk_e, n_used, buf_tok, buf_w):
    n, d = f.shape
    e, _, hdim = w1.shape
    p = buf_tok.shape[0]
    n_blocks = p // MOE_BLOCK
    single = pl.Buffered(1)
    grid_spec = pltpu.PrefetchScalarGridSpec(
        num_scalar_prefetch=3,
        grid=(n_blocks,),
        in_specs=[pl.BlockSpec(memory_space=pl.ANY),
                  pl.BlockSpec((None, d, hdim), lambda i, be, nu, tk: (be[i], 0, 0), pipeline_mode=single),
                  pl.BlockSpec((None, d, hdim), lambda i, be, nu, tk: (be[i], 0, 0), pipeline_mode=single),
                  pl.BlockSpec((None, hdim, d), lambda i, be, nu, tk: (be[i], 0, 0), pipeline_mode=single),
                  pl.BlockSpec((MOE_BLOCK, 1), lambda i, be, nu, tk: (i, 0))],
        out_specs=pl.BlockSpec((MOE_BLOCK, d), lambda i, be, nu, tk: (i, 0)),
        scratch_shapes=[pltpu.VMEM((2, MOE_BLOCK, d), F32), pltpu.SemaphoreType.DMA((2,))],
    )
    return pl.pallas_call(
        _moe_kernel,
        grid_spec=grid_spec,
        out_shape=jax.ShapeDtypeStruct((p, d), F32),
        compiler_params=_cparams(("arbitrary",)),
        name="moe_experts",
    )(blk_e, n_used, buf_tok, f, w1, w3, w2, buf_w.reshape(p, 1))


def _combine_kernel(pos_ref, h_ref, ys_hbm, nw_ref, o_ref, yg_ref, sem):
    i = pl.program_id(0)
    n_steps = pl.num_programs(0)
    tm = h_ref.shape[0]
    rows = yg_ref.shape[1]

    def start(step, slot):
        def body(r, _):
            _row_copy(ys_hbm, yg_ref, sem, pos_ref[step * rows + r], slot, r).start()
            return 0
        lax.fori_loop(0, rows, body, 0, unroll=8)

    def wait(slot):
        def body(r, _):
            _row_copy(ys_hbm, yg_ref, sem, 0, slot, r).wait()
            return 0
        lax.fori_loop(0, rows, body, 0, unroll=8)

    @pl.when(i == 0)
    def _():
        start(0, 0)

    @pl.when(i + 1 < n_steps)
    def _():
        start(i + 1, (i + 1) % 2)

    slot = i % 2
    wait(slot)
    h = h_ref[...]
    for k in range(TOP_K_INNER):
        h = h + yg_ref[slot, k * tm:(k + 1) * tm, :]
    y = h * lax.rsqrt(jnp.mean(h * h, axis=-1, keepdims=True) + RMS_EPS)
    o_ref[...] = y * nw_ref[...]


def _combine(h, ys, pos, norm_w):
    n, d = h.shape
    tm = _tile(n, 128)
    grid_spec = pltpu.PrefetchScalarGridSpec(
        num_scalar_prefetch=1,
        grid=(n // tm,),
        in_specs=[pl.BlockSpec((tm, d), lambda i, ps: (i, 0)),
                  pl.BlockSpec(memory_space=pl.ANY),
                  pl.BlockSpec((1, d), lambda i, ps: (0, 0))],
        out_specs=pl.BlockSpec((tm, d), lambda i, ps: (i, 0)),
        scratch_shapes=[pltpu.VMEM((2, TOP_K_INNER * tm, d), F32), pltpu.SemaphoreType.DMA((2,))],
    )
    return pl.pallas_call(
        _combine_kernel,
        grid_spec=grid_spec,
        out_shape=jax.ShapeDtypeStruct((n, d), F32),
        compiler_params=_cparams(("arbitrary",)),
        name="moe_combine_final_norm",
    )(pos, h, ys, norm_w.reshape(1, d).astype(F32))


def _routing(logits, n_tok, tm_combine):
    ng, epg, kk = N_EXPERT_GROUPS, EXPERTS_PER_GROUP, TOP_K_INNER
    n_exp = ng * epg
    p_group = jax.nn.softmax(logits[:, :ng], axis=-1)
    g_prob, g_sel = lax.top_k(p_group, 1)
    logits_e = logits[:, ng:ng + n_exp].reshape(n_tok, ng, epg)
    logits_in = jnp.take_along_axis(logits_e, g_sel[:, :, None], axis=1)[:, 0]
    top_v, top_i = lax.top_k(logits_in, kk)
    gate = jax.nn.softmax(top_v, axis=-1) * g_prob
    e_idx = (g_sel * epg + top_i).astype(jnp.int32)

    m = n_tok * kk
    flat_e = e_idx.reshape(m)
    flat_tok = jnp.repeat(jnp.arange(n_tok, dtype=jnp.int32), kk)
    flat_w = gate.reshape(m)
    order = jnp.argsort(flat_e, stable=True)
    se, stok, sw = flat_e[order], flat_tok[order], flat_w[order]
    counts = jnp.bincount(flat_e, length=n_exp)
    padded = ((counts + MOE_BLOCK - 1) // MOE_BLOCK) * MOE_BLOCK
    pad_end = jnp.cumsum(padded)
    pad_start = pad_end - padded
    grp_start = jnp.cumsum(counts) - counts
    dest = (pad_start[se] + jnp.arange(m, dtype=jnp.int32) - grp_start[se]).astype(jnp.int32)
    n_blocks = -(-m // MOE_BLOCK) + n_exp
    p = n_blocks * MOE_BLOCK
    buf_tok = jnp.zeros((p,), jnp.int32).at[dest].set(stok)
    buf_w = jnp.zeros((p,), F32).at[dest].set(sw)
    blk_e = jnp.minimum(jnp.searchsorted(pad_end, jnp.arange(n_blocks) * MOE_BLOCK, side='right'),
                        n_exp - 1).astype(jnp.int32)
    n_used = (pad_end[-1] // MOE_BLOCK).astype(jnp.int32).reshape(1)
    pos_flat = jnp.zeros((m,), jnp.int32).at[order].set(dest)
    pos = pos_flat.reshape(n_tok // tm_combine, tm_combine, kk).transpose(0, 2, 1).reshape(m)
    return blk_e, n_used, buf_tok, buf_w, pos


def kernel(x, meta_tokens, norm_mix_w, w_in, s5_a_re, s5_a_im, s5_log_dt, s5_b_re, s5_b_im, s5_c_re, s5_c_im, s5_d, s5_glu_w, s5_glu_b, hg_lb_gamma, hg_norm_w, w_branch_a, w_branch_b, w_out, norm_ffn_w, router_group_w, router_group_b, router_expert_w, router_expert_b, expert_w1, expert_w3, expert_w2, norm_final_w):
    n_batch, seq, d_model = x.shape
    depth = w_in.shape[0]
    n_tok = n_batch * seq
    d_s5 = s5_d.shape[-1]
    d_hg = hg_norm_w.shape[-1]
    T = S5_CHUNK
    assert meta_tokens.shape[0] == N_META == T and seq % GLA_TILE == 0
    jb = d_s5 // LANES
    gb = LANES // S5_GROUP
    n_state = s5_a_re.shape[-1]
    sig = jax.nn.sigmoid
    o_u, o_q, o_i, o_f, o_g, o_ga = 0, d_s5, d_s5 + d_hg, d_s5 + 2 * d_hg, d_s5 + 4 * d_hg, d_s5 + 5 * d_hg

    h = x.reshape(n_tok, d_model)
    h_meta = meta_tokens.astype(x.dtype)
    for l in range(depth):
        assert l == 0, "the meta-token shortcut is only valid for a single layer"
        a = _rmsnorm(h, norm_mix_w[l])
        a_m = _rmsnorm(h_meta, norm_mix_w[l])
        w = w_in[l].astype(BF16)
        proj = functools.partial(_matmul, [(a, w)])
        proj_m = functools.partial(_matmul, [(a_m, w)], out_dtype=F32)

        u_blk = proj(n_cols=d_s5, col_off=o_u, lane_blocked=True, name="proj_u")
        q = proj(n_cols=d_hg, col_off=o_q, epilogue=jax.nn.silu, name="proj_q")
        v = proj(n_cols=d_hg, col_off=o_i, name="proj_i")
        z_fb = proj(n_cols=2 * d_hg, col_off=o_f, out_dtype=F32, name="proj_f")
        g_out = proj(n_cols=d_hg, col_off=o_g, epilogue=jax.nn.silu, name="proj_g")
        gates = proj(n_cols=2 * d_model, col_off=o_ga, epilogue=sig, name="proj_gates")
        u_m = proj_m(n_cols=d_s5, col_off=o_u, name="proj_u_meta")
        v_m = proj_m(n_cols=d_hg, col_off=o_i, out_dtype=BF16, name="proj_i_meta")
        zf_m = proj_m(n_cols=d_hg, col_off=o_f, name="proj_f_meta")

        ec, mc, fc, lam_t = _s5_compact_operators(
            s5_a_re[l].astype(F32), s5_a_im[l].astype(F32), s5_log_dt[l].astype(F32), s5_b_re[l].astype(F32),
            s5_b_im[l].astype(F32), s5_c_re[l].astype(F32), s5_c_im[l].astype(F32), s5_d[l].astype(F32))
        rep_tq = _replicator(T, S5_GROUP, gb)
        rep_dp = _replicator(4, n_state, gb)
        e_mat = _expand(ec, rep_dp, S5_GROUP, n_state, gb, "s5_expand_e")
        m_mat = _expand(mc, rep_tq, S5_GROUP, S5_GROUP, gb, "s5_expand_m")
        f_mat = _expand(fc, rep_tq, n_state, S5_GROUP, gb, "s5_expand_f")
        sw = e_mat.shape[-1]
        u_ch = u_blk.reshape(jb, n_tok // T, T * LANES)
        um = u_m.reshape(T, jb, LANES).transpose(1, 0, 2).reshape(jb, 1, T * LANES)
        um = jnp.pad(um, ((0, 0), (0, 7), (0, 0))).astype(BF16)
        x0 = _matmul([(um, e_mat)], n_cols=sw, out_dtype=F32, name="s5_state_meta")[:, 0:1, :]
        x0 = jnp.concatenate([x0[..., :sw // 2], jnp.zeros_like(x0[..., sw // 2:])], axis=-1)
        s_all = _matmul([(u_ch, e_mat)], n_cols=sw, out_dtype=F32, name="s5_chunk_state")
        x_prev = _s5_scan(s_all, lam_t, x0, n_batch)
        y_blk = _matmul([(u_ch, m_mat), (x_prev, f_mat)], n_cols=T * LANES, name="s5_chunk_out")
        s5_out = _glu(y_blk.reshape(jb, n_tok, LANES), s5_glu_w[l].astype(BF16), s5_glu_b[l])

        lb = jnp.cumsum(jax.nn.softmax(hg_lb_gamma.astype(F32), axis=1), axis=1)[:, l]
        hg_out = _gla(q, v, z_fb, g_out, v_m, zf_m, lb, hg_norm_w[l], n_batch)

        mrg = _merge(s5_out, w_branch_a[l].astype(BF16), hg_out, w_branch_b[l].astype(BF16), gates)
        h = _matmul([(mrg, w_out[l].astype(BF16))], n_cols=d_model, extras=(h,),
                    epilogue=lambda acc, r: acc + r, out_dtype=F32, name="out_proj")

        ng, n_exp = router_group_w.shape[-1], router_expert_w.shape[-1]
        rw = jnp.concatenate([router_group_w[l], router_expert_w[l],
                              jnp.zeros((d_model, ROUTER_COLS - ng - n_exp), F32)], axis=1).astype(F32)
        rb = jnp.concatenate([router_group_b[l], router_expert_b[l],
                              jnp.zeros((ROUTER_COLS - ng - n_exp,), F32)]).reshape(1, ROUTER_COLS).astype(F32)
        f, logits = _router(h, norm_ffn_w[l], rw, rb)
        tm_c = _tile(n_tok, 128)
        blk_e, n_used, buf_tok, buf_w, pos = _routing(logits, n_tok, tm_c)
        ys = _moe(f, expert_w1[l], expert_w3[l], expert_w2[l], blk_e, n_used, buf_tok, buf_w)
    out = _combine(h, ys, pos, norm_final_w)
    return out.reshape(n_batch, seq, d_model)
```

```python
import functools
import math

import jax
import jax.numpy as jnp
from jax import lax
from jax.experimental import pallas as pl
from jax.experimental.pallas import tpu as pltpu

F32 = jnp.float32
BF16 = jnp.bfloat16

N_META = 16
S5_GROUP = 16
S5_STATE = 64
S5_CHUNK = 16
HG_EXPAND = 128
GLA_CHUNK = 64
GLA_TILE = 256
N_EXPERT_GROUPS = 8
EXPERTS_PER_GROUP = 8
TOP_K_INNER = 2
MOE_BLOCK = 256
RMS_EPS = 1e-6
LANES = 128
ROUTER_COLS = 128
VMEM_LIMIT = 56 * 1024 * 1024


def _cparams(sem):
    return pltpu.CompilerParams(dimension_semantics=sem, vmem_limit_bytes=VMEM_LIMIT)


def _tile(n, pref):
    t = min(n, pref)
    assert n % t == 0, (n, pref)
    return t


def _rmsnorm_kernel(x_ref, w_ref, o_ref):
    x = x_ref[...]
    y = x * lax.rsqrt(jnp.mean(x * x, axis=-1, keepdims=True) + RMS_EPS)
    o_ref[...] = (y * w_ref[...]).astype(o_ref.dtype)


def _rmsnorm(x, w, out_dtype=BF16):
    n, d = x.shape
    tm = _tile(n, 256)
    return pl.pallas_call(
        _rmsnorm_kernel,
        grid=(n // tm,),
        in_specs=[pl.BlockSpec((tm, d), lambda i: (i, 0)), pl.BlockSpec((1, d), lambda i: (0, 0))],
        out_specs=pl.BlockSpec((tm, d), lambda i: (i, 0)),
        out_shape=jax.ShapeDtypeStruct((n, d), out_dtype),
        compiler_params=_cparams(("parallel",)),
        name="rmsnorm",
    )(x, w.reshape(1, d).astype(F32))


def _mm_kernel(*refs, n_pairs, n_extra, epilogue, lane_blocked):
    a_refs = refs[0:2 * n_pairs:2]
    w_refs = refs[1:2 * n_pairs:2]
    extras = refs[2 * n_pairs:2 * n_pairs + n_extra]
    o_ref = refs[2 * n_pairs + n_extra]
    acc = None
    for a_ref, w_ref in zip(a_refs, w_refs):
        d = jnp.dot(a_ref[...], w_ref[...], preferred_element_type=F32)
        acc = d if acc is None else acc + d
    res = epilogue(acc, *[e[...] for e in extras]).astype(o_ref.dtype)
    if lane_blocked:
        for q in range(o_ref.shape[0]):
            o_ref[q] = res[:, q * LANES:(q + 1) * LANES]
    else:
        o_ref[...] = res


def _matmul(pairs, *, n_cols, col_off=0, extras=(), epilogue=None, out_dtype=BF16, lane_blocked=False,
            tm=1024, tn=1024, name="matmul"):
    if epilogue is None:
        epilogue = lambda acc: acc
    a0 = pairs[0][0]
    batched = a0.ndim == 3
    g = a0.shape[0] if batched else 1
    m = a0.shape[-2]
    tm = _tile(m, tm)
    tn = _tile(math.gcd(n_cols, col_off) if col_off else n_cols, tn)
    cb = col_off // tn
    lead = (None,) if batched else ()

    def bidx(b):
        return (b,) if batched else ()

    in_specs, args = [], []
    for a, w in pairs:
        kdim = a.shape[-1]
        assert a.dtype == BF16 and w.dtype == BF16 and w.shape[-2] == kdim and a.shape[-2] == m
        in_specs.append(pl.BlockSpec(lead + (tm, kdim), lambda b, i, j: bidx(b) + (i, 0)))
        in_specs.append(pl.BlockSpec(lead + (kdim, tn), lambda b, i, j: bidx(b) + (0, j + cb)))
        args += [a, w]
    for e in extras:
        if e.shape[-2] == 1:
            in_specs.append(pl.BlockSpec((1, tn), lambda b, i, j: (0, j)))
        else:
            in_specs.append(pl.BlockSpec((tm, tn), lambda b, i, j: (i, j)))
        args.append(e)
    if lane_blocked:
        assert not batched
        out_shape = jax.ShapeDtypeStruct((n_cols // LANES, m, LANES), out_dtype)
        out_spec = pl.BlockSpec((tn // LANES, tm, LANES), lambda b, i, j: (j, i, 0))
    else:
        out_shape = jax.ShapeDtypeStruct(((g,) if batched else ()) + (m, n_cols), out_dtype)
        out_spec = pl.BlockSpec(lead + (tm, tn), lambda b, i, j: bidx(b) + (i, j))
    return pl.pallas_call(
        functools.partial(_mm_kernel, n_pairs=len(pairs), n_extra=len(extras), epilogue=epilogue,
                          lane_blocked=lane_blocked),
        grid=(g, m // tm, n_cols // tn),
        in_specs=in_specs,
        out_specs=out_spec,
        out_shape=out_shape,
        compiler_params=_cparams(("parallel", "parallel", "parallel")),
        name=name,
    )(*args)


def _s5_compact_operators(a_re, a_im, log_dt, b_re, b_im, c_re, c_im, d_skip):
    T = S5_CHUNK
    hp = lax.Precision.HIGHEST
    G, P = a_re.shape[1:]
    H = b_re.shape[-1]
    GB = LANES // H
    J = G // GB
    dt = jnp.exp(log_dt)[..., None]
    lre, lang = a_re * dt, a_im * dt
    mag = jnp.exp(lre)
    ab_re, ab_im = mag * jnp.cos(lang), mag * jnp.sin(lang)
    den = a_re * a_re + a_im * a_im
    nr, ni = ab_re - 1.0, ab_im
    z_re = ((nr * a_re + ni * a_im) / den)[..., None]
    z_im = ((ni * a_re - nr * a_im) / den)[..., None]
    bb_re = z_re * b_re - z_im * b_im
    bb_im = z_re * b_im + z_im * b_re
    bbt_re, bbt_im = bb_re.transpose(0, 1, 3, 2), bb_im.transpose(0, 1, 3, 2)
    n = jnp.arange(T + 1, dtype=F32)[:, None]
    pmag = jnp.exp(lre[:, :, None, :] * n)
    pw_re = pmag * jnp.cos(lang[:, :, None, :] * n)
    pw_im = pmag * jnp.sin(lang[:, :, None, :] * n)
    cl_re = c_re[:, :, None] * pw_re[:, :, :, None, :] - c_im[:, :, None] * pw_im[:, :, :, None, :]
    cl_im = c_re[:, :, None] * pw_im[:, :, :, None, :] + c_im[:, :, None] * pw_re[:, :, :, None, :]
    lhs = jnp.concatenate([bbt_re, -bbt_im], axis=-1)
    rhs = jnp.concatenate([cl_re, cl_im], axis=-1).reshape(2, G, (T + 1) * H, 2 * P)
    kk = jnp.einsum('dghk,dgmk->dghm', lhs, rhs, precision=hp).reshape(2, G, H, T + 1, H)
    eye_h = jnp.eye(H, dtype=F32)
    kk_f = kk[0].at[:, :, 0, :].add(eye_h[None] * d_skip.reshape(G, H)[:, :, None])
    kk_b = kk[1]
    rows = []
    for s in range(T):
        fwd = jnp.pad(kk_f[:, :, :T - s], ((0, 0), (0, 0), (s, 0), (0, 0)))
        bwd = jnp.pad(jnp.flip(kk_b[:, :, :s + 1], axis=2), ((0, 0), (0, 0), (0, T - 1 - s), (0, 0)))
        rows.append((fwd + bwd).reshape(G, H, T * H))
    mc = jnp.stack(rows, axis=0).reshape(T, J, GB, H, T * H).transpose(1, 0, 2, 3, 4).reshape(J, T * LANES, T * H)

    ar = jnp.arange(T)
    e_parts = []
    for d, idx in ((0, T - 1 - ar), (1, ar)):
        pr, pi = pw_re[d][:, idx, None, :], pw_im[d][:, idx, None, :]
        br, bi = bbt_re[d][:, None], bbt_im[d][:, None]
        e_parts += [pr * br - pi * bi, pr * bi + pi * br]
    ec = jnp.stack(e_parts, axis=3).reshape(J, GB, T, H, 4 * P).transpose(0, 2, 1, 3, 4).reshape(J, T * LANES, 4 * P)

    ct_re, ct_im = c_re.transpose(0, 1, 3, 2), c_im.transpose(0, 1, 3, 2)
    pt_re, pt_im = pw_re.transpose(0, 1, 3, 2), pw_im.transpose(0, 1, 3, 2)
    f_parts = []
    for d, idx in ((0, ar + 1), (1, T - ar)):
        qr, qi = pt_re[d][:, :, idx, None], pt_im[d][:, :, idx, None]
        cr, ci = ct_re[d][:, :, None, :], ct_im[d][:, :, None, :]
        f_parts += [cr * qr - ci * qi, -(cr * qi + ci * qr)]
    fc = jnp.stack(f_parts, axis=0).reshape(4, J, GB * P, T * H).transpose(1, 0, 2, 3).reshape(J, 4 * GB * P, T * H)

    lam_t = jnp.stack([pw_re[0][:, T], pw_im[0][:, T], pw_re[1][:, T], pw_im[1][:, T]], 0)
    lam_t = lam_t.reshape(4, J, GB * P).transpose(1, 0, 2).reshape(J, 1, 4 * GB * P)
    return ec.astype(BF16), mc.astype(BF16), fc.astype(BF16), lam_t


def _replicator(n_outer, n_inner, reps):
    rows = jnp.arange(n_outer * n_inner)
    cols = jnp.arange(n_outer * reps * n_inner)
    same_o = (rows[:, None] // n_inner) == (cols[None, :] // (reps * n_inner))
    same_i = (rows[:, None] % n_inner) == (cols[None, :] % n_inner)
    return (same_o & same_i).astype(BF16)


def _expand_kernel(c_ref, r_ref, o_ref, *, row_group, col_group, n_groups):
    r, tn = o_ref.shape
    rep = jnp.dot(c_ref[...], r_ref[...], preferred_element_type=F32)
    rows = lax.broadcasted_iota(jnp.int32, (r, 1), 0)
    cols = lax.broadcasted_iota(jnp.int32, (1, tn), 1) + pl.program_id(1) * tn
    row_g = (rows >> int(math.log2(row_group))) & (n_groups - 1)
    col_g = (cols >> int(math.log2(col_group))) & (n_groups - 1)
    o_ref[...] = jnp.where(row_g == col_g, rep, 0.0).astype(o_ref.dtype)


def _expand(compact, rmat, row_group, col_group, n_groups, name):
    j, r, c = compact.shape
    n_cols = rmat.shape[1]
    tn = _tile(n_cols, 1024)
    return pl.pallas_call(
        functools.partial(_expand_kernel, row_group=row_group, col_group=col_group, n_groups=n_groups),
        grid=(j, n_cols // tn),
        in_specs=[pl.BlockSpec((None, r, c), lambda i, k: (i, 0, 0)), pl.BlockSpec((c, tn), lambda i, k: (0, k))],
        out_specs=pl.BlockSpec((None, r, tn), lambda i, k: (i, 0, k)),
        out_shape=jax.ShapeDtypeStruct((j, r, n_cols), BF16),
        compiler_params=_cparams(("parallel", "parallel")),
        name=name,
    )(compact, rmat)


def _s5_scan_kernel(s_ref, lam_ref, x0_ref, o_ref, xp_ref, *, n_batch, n_chunks):
    hw = s_ref.shape[-1] // 4
    lam = lam_ref[...]
    lfr, lfi, lbr, lbi = (lam[:, q * hw:(q + 1) * hw] for q in range(4))
    x0 = x0_ref[...]
    zeros = jnp.zeros((1, hw), F32)
    init = tuple((x0[:, 0:hw], x0[:, hw:2 * hw], zeros, zeros) for _ in range(n_batch))

    def step(c, carry):
        out = []
        for b, (xfr, xfi, xbr, xbi) in enumerate(carry):
            row_f = pl.ds(b * n_chunks + c, 1)
            row_b = pl.ds(b * n_chunks + n_chunks - 1 - c, 1)
            xp_ref[row_f, 0:hw] = xfr
            xp_ref[row_f, hw:2 * hw] = xfi
            xp_ref[row_b, 2 * hw:3 * hw] = xbr
            xp_ref[row_b, 3 * hw:4 * hw] = xbi
            sf = s_ref[row_f, 0:2 * hw]
            sb = s_ref[row_b, 2 * hw:4 * hw]
            out.append((lfr * xfr - lfi * xfi + sf[:, 0:hw],
                        lfr * xfi + lfi * xfr + sf[:, hw:2 * hw],
                        lbr * xbr - lbi * xbi + sb[:, 0:hw],
                        lbr * xbi + lbi * xbr + sb[:, hw:2 * hw]))
        return tuple(out)

    lax.fori_loop(0, n_chunks, step, init)
    o_ref[...] = xp_ref[...].astype(o_ref.dtype)


def _s5_scan(s_all, lam_t, x0, n_batch):
    j, r, w = s_all.shape
    n_chunks = r // n_batch
    return pl.pallas_call(
        functools.partial(_s5_scan_kernel, n_batch=n_batch, n_chunks=n_chunks),
        grid=(j,),
        in_specs=[pl.BlockSpec((None, r, w), lambda i: (i, 0, 0)),
                  pl.BlockSpec((None, 1, w), lambda i: (i, 0, 0)),
                  pl.BlockSpec((None, 1, w), lambda i: (i, 0, 0))],
        out_specs=pl.BlockSpec((None, r, w), lambda i: (i, 0, 0)),
        out_shape=jax.ShapeDtypeStruct((j, r, w), BF16),
        scratch_shapes=[pltpu.VMEM((r, w), F32)],
        compiler_params=_cparams(("parallel",)),
        name="s5_scan",
    )(s_all, lam_t, x0)


def _glu_kernel(y_ref, w_ref, b_ref, o_ref, z_ref):
    jn = pl.program_id(1)
    tn = o_ref.shape[-1]

    @pl.when(jn == 0)
    def _():
        for q in range(y_ref.shape[0]):
            z_ref[:, q * LANES:(q + 1) * LANES] = jax.nn.gelu(y_ref[q].astype(F32)).astype(z_ref.dtype)

    gate = jnp.dot(z_ref[...], w_ref[...], preferred_element_type=F32) + b_ref[...]
    zc = z_ref[:, pl.ds(pl.multiple_of(jn * tn, tn), tn)].astype(F32)
    o_ref[...] = (zc * jax.nn.sigmoid(gate)).astype(o_ref.dtype)


def _glu(y_blk, glu_w, glu_b):
    jb, n, _ = y_blk.shape
    d = jb * LANES
    tm, tn = _tile(n, 1024), _tile(d, 1024)
    return pl.pallas_call(
        _glu_kernel,
        grid=(n // tm, d // tn),
        in_specs=[pl.BlockSpec((jb, tm, LANES), lambda i, j: (0, i, 0)),
                  pl.BlockSpec((d, tn), lambda i, j: (0, j)),
                  pl.BlockSpec((1, tn), lambda i, j: (0, j))],
        out_specs=pl.BlockSpec((tm, tn), lambda i, j: (i, j)),
        out_shape=jax.ShapeDtypeStruct((n, d), BF16),
        scratch_shapes=[pltpu.VMEM((tm, d), BF16)],
        compiler_params=_cparams(("parallel", "arbitrary")),
        name="s5_glu",
    )(y_blk, glu_w, glu_b.reshape(1, d).astype(F32))


def _nt(a, b):
    return lax.dot_general(a, b, (((1,), (1,)), ((), ())), preferred_element_type=F32)


def _split3(x):
    hi = x.astype(BF16)
    r1 = x - hi.astype(F32)
    mid = r1.astype(BF16)
    lo = (r1 - mid.astype(F32)).astype(BF16)
    return jnp.concatenate([hi, mid, lo], axis=0)


def _gates(z, lbd):
    f = lbd + (1.0 - lbd) * jax.nn.sigmoid(z)
    return 1.0 - f, jnp.log(f)


def _gla_kernel(q_ref, v_ref, zf_ref, zb_ref, go_ref, vm_ref, zm_ref, lb_ref, nw_ref, o_ref,
                acc_ref, vt_ref, k_ref, g3_ref, b_ref, qt_ref, kt_ref, qe_ref, kh_ref, u_ref, dl_ref, st_ref):
    L, K = q_ref.shape
    C = GLA_CHUNK
    TR = GLA_TILE
    cpt = TR // C
    n_chunks = L // C
    n_tiles = L // TR
    lb = lb_ref[...]
    shift_c, shift_k = int(math.log2(C)), int(math.log2(K))
    row = lax.broadcasted_iota(jnp.int32, (TR, TR), 0)
    col = lax.broadcasted_iota(jnp.int32, (TR, TR), 1)
    same_chunk = (row >> shift_c) == (col >> shift_c)
    own_block = ((lax.broadcasted_iota(jnp.int32, (TR, cpt * K), 0) >> shift_c)
                 == (lax.broadcasted_iota(jnp.int32, (TR, cpt * K), 1) >> shift_k))

    def crow(c):
        return pl.ds(pl.multiple_of(c * C, C), C)

    def trow(t):
        return pl.ds(pl.multiple_of(t * TR, TR), TR)

    def block_diag(x):
        return jnp.where(own_block, jnp.concatenate([x] * cpt, axis=1), jnp.zeros((), x.dtype))

    def transpose_v(t, _):
        vt_ref[t] = v_ref[trow(t), :].astype(F32).T.astype(BF16)
        return 0

    lax.fori_loop(0, n_tiles, transpose_v, 0, unroll=2)

    m = zm_ref.shape[0]
    km, gm = _gates(zm_ref[...], lb[0:1, :])
    mrow = lax.broadcasted_iota(jnp.int32, (m, m), 0)
    mcol = lax.broadcasted_iota(jnp.int32, (m, m), 1)
    bm = jnp.dot(jnp.concatenate([(mrow >= mcol).astype(BF16)] * 3, axis=1), _split3(gm),
                 preferred_element_type=F32)
    khm = (km * jnp.exp(bm[m - 1:m, :] - bm)).astype(BF16)
    st0 = jnp.dot(vm_ref[...].astype(F32).T.astype(BF16), khm, preferred_element_type=F32)

    for d, z_ref in enumerate((zf_ref, zb_ref)):
        lbd = lb[d:d + 1, :]
        causal = (row >= col) if d == 0 else (col >= row)
        keep = same_chunk & causal
        cum = jnp.concatenate([causal[:C, :C].astype(BF16)] * 3, axis=1)
        ref_row = C // 2 - 1 if d == 0 else C // 2
        edge_row = C - 1 if d == 0 else 0

        def gate_stage(t, _):
            k, g = _gates(z_ref[trow(t), :], lbd)
            k_ref[trow(t), :] = k
            hi = g.astype(BF16)
            r1 = g - hi.astype(F32)
            mid = r1.astype(BF16)
            lo = (r1 - mid.astype(F32)).astype(BF16)
            for part, val in enumerate((hi, mid, lo)):
                for j in range(cpt):
                    g3_ref[t, part * C:(part + 1) * C, j * K:(j + 1) * K] = val[j * C:(j + 1) * C, :]
            return 0

        lax.fori_loop(0, n_tiles, gate_stage, 0, unroll=2)

        def cumsum_stage(t, _):
            bw = jnp.dot(cum, g3_ref[t], preferred_element_type=F32)
            for j in range(cpt):
                b_ref[crow(t * cpt + j), :] = bw[:, j * K:(j + 1) * K]
            return 0

        lax.fori_loop(0, n_tiles, cumsum_stage, 0, unroll=4)

        def scale_stage(c, _):
            rows = crow(c)
            b = b_ref[rows, :]
            r = b[ref_row:ref_row + 1, :]
            e = b[edge_row:edge_row + 1, :]
            qt = q_ref[rows, :].astype(F32) * jnp.exp(b - r)
            kt = k_ref[rows, :] * jnp.exp(r - b)
            qt_ref[rows, :] = qt.astype(BF16)
            kt_ref[rows, :] = kt.astype(BF16)
            qe_ref[rows, :] = (qt * jnp.exp(r)).astype(BF16)
            kh_ref[rows, :] = (kt * jnp.exp(e - r)).astype(BF16)
            dl_ref[pl.ds(c, 1), :] = jnp.exp(e)
            return 0

        lax.fori_loop(0, n_chunks, scale_stage, 0, unroll=2)

        def intra_stage(t, _):
            rows = trow(t)
            s = jnp.where(keep, _nt(qt_ref[rows, :], kt_ref[rows, :]), 0.0).astype(BF16)
            acc_ref[d, rows, :] = jnp.dot(s, v_ref[rows, :], preferred_element_type=F32)
            u_all = jnp.dot(vt_ref[t], block_diag(kh_ref[rows, :]), preferred_element_type=F32)
            for j in range(cpt):
                u_ref[t * cpt + j] = u_all[:, j * K:(j + 1) * K]
            return 0

        lax.fori_loop(0, n_tiles, intra_stage, 0, unroll=2)

        def state_stage(i, st):
            c = i if d == 0 else n_chunks - 1 - i
            st_ref[c] = st.astype(BF16)
            return st * dl_ref[pl.ds(c, 1), :] + u_ref[c]

        lax.fori_loop(0, n_chunks, state_stage, st0 if d == 0 else jnp.zeros((K, K), F32), unroll=4)

        def inter_stage(t, _):
            rows = trow(t)
            st_cat = jnp.concatenate([st_ref[t * cpt + j] for j in range(cpt)], axis=1)
            acc_ref[d, rows, :] += _nt(block_diag(qe_ref[rows, :]), st_cat)
            return 0

        lax.fori_loop(0, n_tiles, inter_stage, 0, unroll=2)

    o = acc_ref[0] + acc_ref[1]
    o = o * lax.rsqrt(jnp.mean(o * o, axis=-1, keepdims=True) + RMS_EPS)
    o_ref[...] = (o * nw_ref[...] * go_ref[...].astype(F32)).astype(o_ref.dtype)


def _gla(q, v, z_fb, go, v_meta, zf_meta, lb, norm_w, n_batch):
    n, d = q.shape
    L = n // n_batch
    heads = d // HG_EXPAND
    m = v_meta.shape[0]
    C = GLA_CHUNK
    n_chunks = L // C
    hk = HG_EXPAND
    tok = pl.BlockSpec((L, hk), lambda b, h: (b, h))
    tok_bwd = pl.BlockSpec((L, hk), lambda b, h: (b, h + heads))
    return pl.pallas_call(
        _gla_kernel,
        grid=(n_batch, heads),
        in_specs=[tok, tok, tok, tok_bwd, tok,
                  pl.BlockSpec((m, hk), lambda b, h: (0, h)),
                  pl.BlockSpec((m, hk), lambda b, h: (0, h)),
                  pl.BlockSpec((2, hk), lambda b, h: (0, h)),
                  pl.BlockSpec((1, hk), lambda b, h: (0, h))],
        out_specs=tok,
        out_shape=jax.ShapeDtypeStruct((n, d), BF16),
        scratch_shapes=[pltpu.VMEM((2, L, hk), F32),
                        pltpu.VMEM((L // GLA_TILE, hk, GLA_TILE), BF16),
                        pltpu.VMEM((L, hk), F32),
                        pltpu.VMEM((L // GLA_TILE, 3 * C, (GLA_TILE // C) * hk), BF16),
                        pltpu.VMEM((L, hk), F32),
                        pltpu.VMEM((L, hk), BF16),
                        pltpu.VMEM((L, hk), BF16),
                        pltpu.VMEM((L, hk), BF16),
                        pltpu.VMEM((L, hk), BF16),
                        pltpu.VMEM((n_chunks, hk, hk), F32),
                        pltpu.VMEM((n_chunks, hk), F32),
                        pltpu.VMEM((n_chunks, hk, hk), BF16)],
        compiler_params=_cparams(("parallel", "parallel")),
        name="hgrn2_gla",
    )(q, v, z_fb, z_fb, go, v_meta, zf_meta, lb, norm_w.reshape(1, d).astype(F32))


def _merge_kernel(a1_ref, w1_ref, a2_ref, w2_ref, g1_ref, g2_ref, o_ref):
    y1 = jnp.dot(a1_ref[...], w1_ref[...], preferred_element_type=F32)
    y2 = jnp.dot(a2_ref[...], w2_ref[...], preferred_element_type=F32)
    o_ref[...] = (g1_ref[...].astype(F32) * y1 + g2_ref[...].astype(F32) * y2).astype(o_ref.dtype)


def _merge(a1, w1, a2, w2, gates):
    n, k1 = a1.shape
    k2 = a2.shape[1]
    d = w1.shape[1]
    tm, tn = _tile(n, 1024), _tile(d, 1024)
    nj = d // tn
    o_spec = pl.BlockSpec((tm, tn), lambda i, j: (i, j))
    return pl.pallas_call(
        _merge_kernel,
        grid=(n // tm, nj),
        in_specs=[pl.BlockSpec((tm, k1), lambda i, j: (i, 0)), pl.BlockSpec((k1, tn), lambda i, j: (0, j)),
                  pl.BlockSpec((tm, k2), lambda i, j: (i, 0)), pl.BlockSpec((k2, tn), lambda i, j: (0, j)),
                  o_spec, pl.BlockSpec((tm, tn), lambda i, j: (i, j + nj))],
        out_specs=o_spec,
        out_shape=jax.ShapeDtypeStruct((n, d), BF16),
        compiler_params=_cparams(("parallel", "parallel")),
        name="branch_merge",
    )(a1, w1, a2, w2, gates, gates)


def _router_kernel(h_ref, nw_ref, rw_ref, rb_ref, f_ref, rt_ref, *, n_groups, per_group):
    x = h_ref[...]
    f = x * lax.rsqrt(jnp.mean(x * x, axis=-1, keepdims=True) + RMS_EPS) * nw_ref[...]
    f_ref[...] = f
    logits = jnp.dot(f, rw_ref[...], precision=lax.Precision.HIGHEST, preferred_element_type=F32) + rb_ref[...]
    lane = lax.broadcasted_iota(jnp.int32, logits.shape, 1)
    lane_f = lane.astype(F32)
    big = float(logits.shape[1])
    neg = -jnp.inf

    def first_argmax(v):
        vmax = jnp.max(v, axis=-1, keepdims=True)
        return vmax, jnp.min(jnp.where(v == vmax, lane_f, big), axis=-1, keepdims=True)

    lg = jnp.where(lane < n_groups, logits, neg)
    g_max, g_sel = first_argmax(lg)
    g_prob = 1.0 / jnp.sum(jnp.exp(lg - g_max), axis=-1, keepdims=True)
    lo = n_groups + g_sel * per_group
    le = jnp.where((lane_f >= lo) & (lane_f < lo + per_group), logits, neg)
    v1, i1 = first_argmax(le)
    v2, i2 = first_argmax(jnp.where(lane_f == i1, neg, le))
    t = jnp.exp(v2 - v1)
    w1 = g_prob / (1.0 + t)
    w2 = g_prob * t / (1.0 + t)
    rt = jnp.where(lane == 0, i1 - n_groups, jnp.where(lane == 1, i2 - n_groups,
                   jnp.where(lane == 2, w1, jnp.where(lane == 3, w2, 0.0))))
    rt_ref[...] = rt


def _router(h, norm_w, rw, rb, n_groups, per_group):
    n, d = h.shape
    tm = _tile(n, 256)
    return pl.pallas_call(
        functools.partial(_router_kernel, n_groups=n_groups, per_group=per_group),
        grid=(n // tm,),
        in_specs=[pl.BlockSpec((tm, d), lambda i: (i, 0)), pl.BlockSpec((1, d), lambda i: (0, 0)),
                  pl.BlockSpec((d, ROUTER_COLS), lambda i: (0, 0)),
                  pl.BlockSpec((1, ROUTER_COLS), lambda i: (0, 0))],
        out_specs=[pl.BlockSpec((tm, d), lambda i: (i, 0)), pl.BlockSpec((tm, ROUTER_COLS), lambda i: (i, 0))],
        out_shape=[jax.ShapeDtypeStruct((n, d), F32), jax.ShapeDtypeStruct((n, ROUTER_COLS), F32)],
        compiler_params=_cparams(("parallel",)),
        name="ffn_norm_router",
    )(h, norm_w.reshape(1, d).astype(F32), rw, rb)


def _row_copy(src_hbm, dst_vmem, sem, src_row, slot, dst_row):
    return pltpu.make_async_copy(src_hbm.at[pl.ds(src_row, 1)], dst_vmem.at[slot, pl.ds(dst_row, 1)], sem.at[slot])


def _moe_kernel(blk_e_ref, n_used_ref, tok_ref, f_hbm, w1_ref, w3_ref, w2_ref, gw_ref, o_ref, xs_ref, sem):
    del blk_e_ref
    i = pl.program_id(0)
    n_used = n_used_ref[0]
    rows = xs_ref.shape[1]

    def start(blk, slot):
        def body(r, _):
            _row_copy(f_hbm, xs_ref, sem, tok_ref[blk * rows + r], slot, r).start()
            return 0
        lax.fori_loop(0, rows, body, 0, unroll=8)

    def wait(slot):
        def body(r, _):
            _row_copy(f_hbm, xs_ref, sem, 0, slot, r).wait()
            return 0
        lax.fori_loop(0, rows, body, 0, unroll=8)

    @pl.when(jnp.logical_and(i == 0, n_used > 0))
    def _():
        start(0, 0)

    @pl.when(i + 1 < n_used)
    def _():
        start(i + 1, (i + 1) % 2)

    @pl.when(i < n_used)
    def _():
        slot = i % 2
        wait(slot)
        x = xs_ref[slot].astype(BF16)
        h1 = jnp.dot(x, w1_ref[...].astype(BF16), preferred_element_type=F32)
        h3 = jnp.dot(x, w3_ref[...].astype(BF16), preferred_element_type=F32)
        hb = (jax.nn.silu(h1) * h3).astype(BF16)
        y = jnp.dot(hb, w2_ref[...].astype(BF16), preferred_element_type=F32)
        o_ref[...] = (y * gw_ref[...]).astype(o_ref.dtype)

    @pl.when(i >= n_used)
    def _():
        o_ref[...] = jnp.zeros_like(o_ref)


def _moe(f, w1, w3, w2, blk_e, n_used, buf_tok, buf_w):
    n, d = f.shape
    e, _, hdim = w1.shape
    p = buf_tok.shape[0]
    n_blocks = p // MOE_BLOCK
    single = pl.Buffered(1)
    grid_spec = pltpu.PrefetchScalarGridSpec(
        num_scalar_prefetch=3,
        grid=(n_blocks,),
        in_specs=[pl.BlockSpec(memory_space=pl.ANY),
                  pl.BlockSpec((None, d, hdim), lambda i, be, nu, tk: (be[i], 0, 0), pipeline_mode=single),
                  pl.BlockSpec((None, d, hdim), lambda i, be, nu, tk: (be[i], 0, 0), pipeline_mode=single),
                  pl.BlockSpec((None, hdim, d), lambda i, be, nu, tk: (be[i], 0, 0), pipeline_mode=single),
                  pl.BlockSpec((MOE_BLOCK, 1), lambda i, be, nu, tk: (i, 0))],
        out_specs=pl.BlockSpec((MOE_BLOCK, d), lambda i, be, nu, tk: (i, 0)),
        scratch_shapes=[pltpu.VMEM((2, MOE_BLOCK, d), F32), pltpu.SemaphoreType.DMA((2,))],
    )
    return pl.pallas_call(
        _moe_kernel,
        grid_spec=grid_spec,
        out_shape=jax.ShapeDtypeStruct((p, d), F32),
        compiler_params=_cparams(("arbitrary",)),
        name="moe_experts",
    )(blk_e, n_used, buf_tok, f, w1, w3, w2, buf_w.reshape(p, 1))


def _combine_kernel(pos_ref, h_ref, ys_hbm, nw_ref, o_ref, yg_ref, sem):
    i = pl.program_id(0)
    n_steps = pl.num_programs(0)
    tm = h_ref.shape[0]
    rows = yg_ref.shape[1]

    def start(step, slot):
        def body(r, _):
            _row_copy(ys_hbm, yg_ref, sem, pos_ref[step * rows + r], slot, r).start()
            return 0
        lax.fori_loop(0, rows, body, 0, unroll=8)

    def wait(slot):
        def body(r, _):
            _row_copy(ys_hbm, yg_ref, sem, 0, slot, r).wait()
            return 0
        lax.fori_loop(0, rows, body, 0, unroll=8)

    @pl.when(i == 0)
    def _():
        start(0, 0)

    @pl.when(i + 1 < n_steps)
    def _():
        start(i + 1, (i + 1) % 2)

    slot = i % 2
    wait(slot)
    h = h_ref[...]
    for k in range(TOP_K_INNER):
        h = h + yg_ref[slot, k * tm:(k + 1) * tm, :]
    y = h * lax.rsqrt(jnp.mean(h * h, axis=-1, keepdims=True) + RMS_EPS)
    o_ref[...] = y * nw_ref[...]


def _combine(h, ys, pos, norm_w):
    n, d = h.shape
    tm = _tile(n, 128)
    grid_spec = pltpu.PrefetchScalarGridSpec(
        num_scalar_prefetch=1,
        grid=(n // tm,),
        in_specs=[pl.BlockSpec((tm, d), lambda i, ps: (i, 0)),
                  pl.BlockSpec(memory_space=pl.ANY),
                  pl.BlockSpec((1, d), lambda i, ps: (0, 0))],
        out_specs=pl.BlockSpec((tm, d), lambda i, ps: (i, 0)),
        scratch_shapes=[pltpu.VMEM((2, TOP_K_INNER * tm, d), F32), pltpu.SemaphoreType.DMA((2,))],
    )
    return pl.pallas_call(
        _combine_kernel,
        grid_spec=grid_spec,
        out_shape=jax.ShapeDtypeStruct((n, d), F32),
        compiler_params=_cparams(("arbitrary",)),
        name="moe_combine_final_norm",
    )(pos, h, ys, norm_w.reshape(1, d).astype(F32))


def _routing(route, n_tok, tm_combine):
    kk = TOP_K_INNER
    n_exp = N_EXPERT_GROUPS * EXPERTS_PER_GROUP
    blk = MOE_BLOCK
    m = n_tok * kk
    flat_e = route[:, :kk].astype(jnp.int32).reshape(m)
    flat_w = route[:, kk:2 * kk].reshape(m)
    iota_m = jnp.arange(m, dtype=jnp.int32)
    se, order, sw = lax.sort((flat_e, iota_m, flat_w), num_keys=1, is_stable=True)
    counts = jnp.sum((flat_e[:, None] == jnp.arange(n_exp, dtype=jnp.int32)[None, :]).astype(jnp.int32), axis=0)
    padded = ((counts + blk - 1) // blk) * blk
    pad_end = jnp.cumsum(padded)
    pad_start = pad_end - padded
    grp_start = jnp.cumsum(counts) - counts
    n_blocks = -(-m // blk) + n_exp
    p = n_blocks * blk
    blk_e = jnp.minimum(jnp.searchsorted(pad_end, jnp.arange(n_blocks, dtype=jnp.int32) * blk, side='right'),
                        n_exp - 1).astype(jnp.int32)
    n_used = (pad_end[-1] // blk).astype(jnp.int32).reshape(1)
    off = jnp.arange(p, dtype=jnp.int32).reshape(n_blocks, blk) - pad_start[blk_e][:, None]
    valid = off < counts[blk_e][:, None]
    src = jnp.clip(grp_start[blk_e][:, None] + off, 0, m - 1)
    buf_tok = jnp.where(valid, order[src] // kk, 0).reshape(p).astype(jnp.int32)
    buf_w = jnp.where(valid, sw[src], 0.0).reshape(p)
    dest = (pad_start[se] + iota_m - grp_start[se]).astype(jnp.int32)
    _, pos_flat = lax.sort((order, dest), num_keys=1)
    pos = pos_flat.reshape(n_tok // tm_combine, tm_combine, kk).transpose(0, 2, 1).reshape(m)
    return blk_e, n_used, buf_tok, buf_w, pos


def kernel(x, meta_tokens, norm_mix_w, w_in, s5_a_re, s5_a_im, s5_log_dt, s5_b_re, s5_b_im, s5_c_re, s5_c_im, s5_d, s5_glu_w, s5_glu_b, hg_lb_gamma, hg_norm_w, w_branch_a, w_branch_b, w_out, norm_ffn_w, router_group_w, router_group_b, router_expert_w, router_expert_b, expert_w1, expert_w3, expert_w2, norm_final_w):
    n_batch, seq, d_model = x.shape
    depth = w_in.shape[0]
    n_tok = n_batch * seq
    d_s5 = s5_d.shape[-1]
    d_hg = hg_norm_w.shape[-1]
    T = S5_CHUNK
    assert meta_tokens.shape[0] == N_META == T and seq % GLA_TILE == 0
    jb = d_s5 // LANES
    gb = LANES // S5_GROUP
    n_state = s5_a_re.shape[-1]
    sig = jax.nn.sigmoid
    o_u, o_q, o_i, o_f, o_g, o_ga = 0, d_s5, d_s5 + d_hg, d_s5 + 2 * d_hg, d_s5 + 4 * d_hg, d_s5 + 5 * d_hg

    h = x.reshape(n_tok, d_model)
    h_meta = meta_tokens.astype(x.dtype)
    for l in range(depth):
        assert l == 0, "the meta-token shortcut is only valid for a single layer"
        a = _rmsnorm(h, norm_mix_w[l])
        a_m = _rmsnorm(h_meta, norm_mix_w[l])
        w = w_in[l].astype(BF16)
        proj = functools.partial(_matmul, [(a, w)])
        proj_m = functools.partial(_matmul, [(a_m, w)], out_dtype=F32)

        u_blk = proj(n_cols=d_s5, col_off=o_u, lane_blocked=True, name="proj_u")
        q = proj(n_cols=d_hg, col_off=o_q, epilogue=jax.nn.silu, name="proj_q")
        v = proj(n_cols=d_hg, col_off=o_i, name="proj_i")
        z_fb = proj(n_cols=2 * d_hg, col_off=o_f, out_dtype=F32, name="proj_f")
        g_out = proj(n_cols=d_hg, col_off=o_g, epilogue=jax.nn.silu, name="proj_g")
        gates = proj(n_cols=2 * d_model, col_off=o_ga, epilogue=sig, name="proj_gates")
        u_m = proj_m(n_cols=d_s5, col_off=o_u, name="proj_u_meta")
        v_m = proj_m(n_cols=d_hg, col_off=o_i, out_dtype=BF16, name="proj_i_meta")
        zf_m = proj_m(n_cols=d_hg, col_off=o_f, name="proj_f_meta")

        ec, mc, fc, lam_t = _s5_compact_operators(
            s5_a_re[l].astype(F32), s5_a_im[l].astype(F32), s5_log_dt[l].astype(F32), s5_b_re[l].astype(F32),
            s5_b_im[l].astype(F32), s5_c_re[l].astype(F32), s5_c_im[l].astype(F32), s5_d[l].astype(F32))
        rep_tq = _replicator(T, S5_GROUP, gb)
        rep_dp = _replicator(4, n_state, gb)
        e_mat = _expand(ec, rep_dp, S5_GROUP, n_state, gb, "s5_expand_e")
        m_mat = _expand(mc, rep_tq, S5_GROUP, S5_GROUP, gb, "s5_expand_m")
        f_mat = _expand(fc, rep_tq, n_state, S5_GROUP, gb, "s5_expand_f")
        sw = e_mat.shape[-1]
        u_ch = u_blk.reshape(jb, n_tok // T, T * LANES)
        um = u_m.reshape(T, jb, LANES).transpose(1, 0, 2).reshape(jb, 1, T * LANES)
        um = jnp.pad(um, ((0, 0), (0, 7), (0, 0))).astype(BF16)
        x0 = _matmul([(um, e_mat)], n_cols=sw, out_dtype=F32, name="s5_state_meta")[:, 0:1, :]
        x0 = jnp.concatenate([x0[..., :sw // 2], jnp.zeros_like(x0[..., sw // 2:])], axis=-1)
        s_all = _matmul([(u_ch, e_mat)], n_cols=sw, out_dtype=F32, name="s5_chunk_state")
        x_prev = _s5_scan(s_all, lam_t, x0, n_batch)
        y_blk = _matmul([(u_ch, m_mat), (x_prev, f_mat)], n_cols=T * LANES, name="s5_chunk_out")
        s5_out = _glu(y_blk.reshape(jb, n_tok, LANES), s5_glu_w[l].astype(BF16), s5_glu_b[l])

        lb = jnp.cumsum(jax.nn.softmax(hg_lb_gamma.astype(F32), axis=1), axis=1)[:, l]
        hg_out = _gla(q, v, z_fb, g_out, v_m, zf_m, lb, hg_norm_w[l], n_batch)

        mrg = _merge(s5_out, w_branch_a[l].astype(BF16), hg_out, w_branch_b[l].astype(BF16), gates)
        h = _matmul([(mrg, w_out[l].astype(BF16))], n_cols=d_model, extras=(h,),
                    epilogue=lambda acc, r: acc + r, out_dtype=F32, name="out_proj")

        ng, n_exp = router_group_w.shape[-1], router_expert_w.shape[-1]
        rw = jnp.concatenate([router_group_w[l], router_expert_w[l],
                              jnp.zeros((d_model, ROUTER_COLS - ng - n_exp), F32)], axis=1).astype(F32)
        rb = jnp.concatenate([router_group_b[l], router_expert_b[l],
                              jnp.zeros((ROUTER_COLS - ng - n_exp,), F32)]).reshape(1, ROUTER_COLS).astype(F32)
        assert (ng, n_exp) == (N_EXPERT_GROUPS, N_EXPERT_GROUPS * EXPERTS_PER_GROUP)
        f, route = _router(h, norm_ffn_w[l], rw, rb, ng, n_exp // ng)
        tm_c = _tile(n_tok, 128)
        blk_e, n_used, buf_tok, buf_w, pos = _routing(route, n_tok, tm_c)
        ys = _moe(f, expert_w1[l], expert_w3[l], expert_w2[l], blk_e, n_used, buf_tok, buf_w)
    out = _combine(h, ys, pos, norm_final_w)
    return out.reshape(n_batch, seq, d_model)
```

```python
import functools
import math

import jax
import jax.numpy as jnp
from jax import lax
from jax.experimental import pallas as pl
from jax.experimental.pallas import tpu as pltpu

F32 = jnp.float32
BF16 = jnp.bfloat16

N_META = 16
S5_GROUP = 16
S5_STATE = 64
S5_CHUNK = 16
HG_EXPAND = 128
GLA_CHUNK = 64
GLA_TILE = 256
N_EXPERT_GROUPS = 8
EXPERTS_PER_GROUP = 8
TOP_K_INNER = 2
MOE_BLOCK = 256
RMS_EPS = 1e-6
LANES = 128
ROUTER_COLS = 128
VMEM_LIMIT = 56 * 1024 * 1024


def _cparams(sem):
    return pltpu.CompilerParams(dimension_semantics=sem, vmem_limit_bytes=VMEM_LIMIT)


def _tile(n, pref):
    t = min(n, pref)
    assert n % t == 0, (n, pref)
    return t


def _rmsnorm_kernel(x_ref, w_ref, o_ref):
    x = x_ref[...]
    y = x * lax.rsqrt(jnp.mean(x * x, axis=-1, keepdims=True) + RMS_EPS)
    o_ref[...] = (y * w_ref[...]).astype(o_ref.dtype)


def _rmsnorm(x, w, out_dtype=BF16):
    n, d = x.shape
    tm = _tile(n, 256)
    return pl.pallas_call(
        _rmsnorm_kernel,
        grid=(n // tm,),
        in_specs=[pl.BlockSpec((tm, d), lambda i: (i, 0)), pl.BlockSpec((1, d), lambda i: (0, 0))],
        out_specs=pl.BlockSpec((tm, d), lambda i: (i, 0)),
        out_shape=jax.ShapeDtypeStruct((n, d), out_dtype),
        compiler_params=_cparams(("parallel",)),
        name="rmsnorm",
    )(x, w.reshape(1, d).astype(F32))


def _mm_kernel(*refs, n_pairs, n_extra, epilogue, lane_blocked):
    a_refs = refs[0:2 * n_pairs:2]
    w_refs = refs[1:2 * n_pairs:2]
    extras = refs[2 * n_pairs:2 * n_pairs + n_extra]
    o_ref = refs[2 * n_pairs + n_extra]
    acc = None
    for a_ref, w_ref in zip(a_refs, w_refs):
        d = jnp.dot(a_ref[...], w_ref[...], preferred_element_type=F32)
        acc = d if acc is None else acc + d
    res = epilogue(acc, *[e[...] for e in extras]).astype(o_ref.dtype)
    if lane_blocked:
        for q in range(o_ref.shape[0]):
            o_ref[q] = res[:, q * LANES:(q + 1) * LANES]
    else:
        o_ref[...] = res


def _matmul(pairs, *, n_cols, col_off=0, extras=(), epilogue=None, out_dtype=BF16, lane_blocked=False,
            tm=1024, tn=1024, name="matmul"):
    if epilogue is None:
        epilogue = lambda acc: acc
    a0 = pairs[0][0]
    batched = a0.ndim == 3
    g = a0.shape[0] if batched else 1
    m = a0.shape[-2]
    tm = _tile(m, tm)
    tn = _tile(math.gcd(n_cols, col_off) if col_off else n_cols, tn)
    cb = col_off // tn
    lead = (None,) if batched else ()

    def bidx(b):
        return (b,) if batched else ()

    in_specs, args = [], []
    for a, w in pairs:
        kdim = a.shape[-1]
        assert a.dtype == BF16 and w.dtype == BF16 and w.shape[-2] == kdim and a.shape[-2] == m
        in_specs.append(pl.BlockSpec(lead + (tm, kdim), lambda b, i, j: bidx(b) + (i, 0)))
        in_specs.append(pl.BlockSpec(lead + (kdim, tn), lambda b, i, j: bidx(b) + (0, j + cb)))
        args += [a, w]
    for e in extras:
        if e.shape[-2] == 1:
            in_specs.append(pl.BlockSpec((1, tn), lambda b, i, j: (0, j)))
        else:
            in_specs.append(pl.BlockSpec((tm, tn), lambda b, i, j: (i, j)))
        args.append(e)
    if lane_blocked:
        assert not batched
        out_shape = jax.ShapeDtypeStruct((n_cols // LANES, m, LANES), out_dtype)
        out_spec = pl.BlockSpec((tn // LANES, tm, LANES), lambda b, i, j: (j, i, 0))
    else:
        out_shape = jax.ShapeDtypeStruct(((g,) if batched else ()) + (m, n_cols), out_dtype)
        out_spec = pl.BlockSpec(lead + (tm, tn), lambda b, i, j: bidx(b) + (i, j))
    return pl.pallas_call(
        functools.partial(_mm_kernel, n_pairs=len(pairs), n_extra=len(extras), epilogue=epilogue,
                          lane_blocked=lane_blocked),
        grid=(g, m // tm, n_cols // tn),
        in_specs=in_specs,
        out_specs=out_spec,
        out_shape=out_shape,
        compiler_params=_cparams(("parallel", "parallel", "parallel")),
        name=name,
    )(*args)


def _s5_compact_operators(a_re, a_im, log_dt, b_re, b_im, c_re, c_im, d_skip):
    T = S5_CHUNK
    hp = lax.Precision.HIGHEST
    G, P = a_re.shape[1:]
    H = b_re.shape[-1]
    GB = LANES // H
    J = G // GB
    dt = jnp.exp(log_dt)[..., None]
    lre, lang = a_re * dt, a_im * dt
    mag = jnp.exp(lre)
    ab_re, ab_im = mag * jnp.cos(lang), mag * jnp.sin(lang)
    den = a_re * a_re + a_im * a_im
    nr, ni = ab_re - 1.0, ab_im
    z_re = ((nr * a_re + ni * a_im) / den)[..., None]
    z_im = ((ni * a_re - nr * a_im) / den)[..., None]
    bb_re = z_re * b_re - z_im * b_im
    bb_im = z_re * b_im + z_im * b_re
    bbt_re, bbt_im = bb_re.transpose(0, 1, 3, 2), bb_im.transpose(0, 1, 3, 2)
    n = jnp.arange(T + 1, dtype=F32)[:, None]
    pmag = jnp.exp(lre[:, :, None, :] * n)
    pw_re = pmag * jnp.cos(lang[:, :, None, :] * n)
    pw_im = pmag * jnp.sin(lang[:, :, None, :] * n)
    cl_re = c_re[:, :, None] * pw_re[:, :, :, None, :] - c_im[:, :, None] * pw_im[:, :, :, None, :]
    cl_im = c_re[:, :, None] * pw_im[:, :, :, None, :] + c_im[:, :, None] * pw_re[:, :, :, None, :]
    lhs = jnp.concatenate([bbt_re, -bbt_im], axis=-1)
    rhs = jnp.concatenate([cl_re, cl_im], axis=-1).reshape(2, G, (T + 1) * H, 2 * P)
    kk = jnp.einsum('dghk,dgmk->dghm', lhs, rhs, precision=hp).reshape(2, G, H, T + 1, H)
    eye_h = jnp.eye(H, dtype=F32)
    kk_f = kk[0].at[:, :, 0, :].add(eye_h[None] * d_skip.reshape(G, H)[:, :, None])
    kk_b = kk[1]
    rows = []
    for s in range(T):
        fwd = jnp.pad(kk_f[:, :, :T - s], ((0, 0), (0, 0), (s, 0), (0, 0)))
        bwd = jnp.pad(jnp.flip(kk_b[:, :, :s + 1], axis=2), ((0, 0), (0, 0), (0, T - 1 - s), (0, 0)))
        rows.append((fwd + bwd).reshape(G, H, T * H))
    mc = jnp.stack(rows, axis=0).reshape(T, J, GB, H, T * H).transpose(1, 0, 2, 3, 4).reshape(J, T * LANES, T * H)

    ar = jnp.arange(T)
    e_parts = []
    for d, idx in ((0, T - 1 - ar), (1, ar)):
        pr, pi = pw_re[d][:, idx, None, :], pw_im[d][:, idx, None, :]
        br, bi = bbt_re[d][:, None], bbt_im[d][:, None]
        e_parts += [pr * br - pi * bi, pr * bi + pi * br]
    ec = jnp.stack(e_parts, axis=3).reshape(J, GB, T, H, 4 * P).transpose(0, 2, 1, 3, 4).reshape(J, T * LANES, 4 * P)

    ct_re, ct_im = c_re.transpose(0, 1, 3, 2), c_im.transpose(0, 1, 3, 2)
    pt_re, pt_im = pw_re.transpose(0, 1, 3, 2), pw_im.transpose(0, 1, 3, 2)
    f_parts = []
    for d, idx in ((0, ar + 1), (1, T - ar)):
        qr, qi = pt_re[d][:, :, idx, None], pt_im[d][:, :, idx, None]
        cr, ci = ct_re[d][:, :, None, :], ct_im[d][:, :, None, :]
        f_parts += [cr * qr - ci * qi, -(cr * qi + ci * qr)]
    fc = jnp.stack(f_parts, axis=0).reshape(4, J, GB * P, T * H).transpose(1, 0, 2, 3).reshape(J, 4 * GB * P, T * H)

    lam_t = jnp.stack([pw_re[0][:, T], pw_im[0][:, T], pw_re[1][:, T], pw_im[1][:, T]], 0)
    lam_t = lam_t.reshape(4, J, GB * P).transpose(1, 0, 2).reshape(J, 1, 4 * GB * P)
    return ec.astype(BF16), mc.astype(BF16), fc.astype(BF16), lam_t


def _replicator(n_outer, n_inner, reps):
    rows = jnp.arange(n_outer * n_inner)
    cols = jnp.arange(n_outer * reps * n_inner)
    same_o = (rows[:, None] // n_inner) == (cols[None, :] // (reps * n_inner))
    same_i = (rows[:, None] % n_inner) == (cols[None, :] % n_inner)
    return (same_o & same_i).astype(BF16)


def _expand_kernel(c_ref, r_ref, o_ref, *, row_group, col_group, n_groups):
    r, tn = o_ref.shape
    rep = jnp.dot(c_ref[...], r_ref[...], preferred_element_type=F32)
    rows = lax.broadcasted_iota(jnp.int32, (r, 1), 0)
    cols = lax.broadcasted_iota(jnp.int32, (1, tn), 1) + pl.program_id(1) * tn
    row_g = (rows >> int(math.log2(row_group))) & (n_groups - 1)
    col_g = (cols >> int(math.log2(col_group))) & (n_groups - 1)
    o_ref[...] = jnp.where(row_g == col_g, rep, 0.0).astype(o_ref.dtype)


def _expand(compact, rmat, row_group, col_group, n_groups, name):
    j, r, c = compact.shape
    n_cols = rmat.shape[1]
    tn = _tile(n_cols, 1024)
    return pl.pallas_call(
        functools.partial(_expand_kernel, row_group=row_group, col_group=col_group, n_groups=n_groups),
        grid=(j, n_cols // tn),
        in_specs=[pl.BlockSpec((None, r, c), lambda i, k: (i, 0, 0)), pl.BlockSpec((c, tn), lambda i, k: (0, k))],
        out_specs=pl.BlockSpec((None, r, tn), lambda i, k: (i, 0, k)),
        out_shape=jax.ShapeDtypeStruct((j, r, n_cols), BF16),
        compiler_params=_cparams(("parallel", "parallel")),
        name=name,
    )(compact, rmat)


def _s5_scan_kernel(s_ref, lam_ref, x0_ref, o_ref, xp_ref, *, n_batch, n_chunks):
    hw = s_ref.shape[-1] // 4
    lam = lam_ref[...]
    lfr, lfi, lbr, lbi = (lam[:, q * hw:(q + 1) * hw] for q in range(4))
    x0 = x0_ref[...]
    zeros = jnp.zeros((1, hw), F32)
    init = tuple((x0[:, 0:hw], x0[:, hw:2 * hw], zeros, zeros) for _ in range(n_batch))

    def step(c, carry):
        out = []
        for b, (xfr, xfi, xbr, xbi) in enumerate(carry):
            row_f = pl.ds(b * n_chunks + c, 1)
            row_b = pl.ds(b * n_chunks + n_chunks - 1 - c, 1)
            xp_ref[row_f, 0:hw] = xfr
            xp_ref[row_f, hw:2 * hw] = xfi
            xp_ref[row_b, 2 * hw:3 * hw] = xbr
            xp_ref[row_b, 3 * hw:4 * hw] = xbi
            sf = s_ref[row_f, 0:2 * hw]
            sb = s_ref[row_b, 2 * hw:4 * hw]
            out.append((lfr * xfr - lfi * xfi + sf[:, 0:hw],
                        lfr * xfi + lfi * xfr + sf[:, hw:2 * hw],
                        lbr * xbr - lbi * xbi + sb[:, 0:hw],
                        lbr * xbi + lbi * xbr + sb[:, hw:2 * hw]))
        return tuple(out)

    lax.fori_loop(0, n_chunks, step, init)
    o_ref[...] = xp_ref[...].astype(o_ref.dtype)


def _s5_scan(s_all, lam_t, x0, n_batch):
    j, r, w = s_all.shape
    n_chunks = r // n_batch
    return pl.pallas_call(
        functools.partial(_s5_scan_kernel, n_batch=n_batch, n_chunks=n_chunks),
        grid=(j,),
        in_specs=[pl.BlockSpec((None, r, w), lambda i: (i, 0, 0)),
                  pl.BlockSpec((None, 1, w), lambda i: (i, 0, 0)),
                  pl.BlockSpec((None, 1, w), lambda i: (i, 0, 0))],
        out_specs=pl.BlockSpec((None, r, w), lambda i: (i, 0, 0)),
        out_shape=jax.ShapeDtypeStruct((j, r, w), BF16),
        scratch_shapes=[pltpu.VMEM((r, w), F32)],
        compiler_params=_cparams(("parallel",)),
        name="s5_scan",
    )(s_all, lam_t, x0)


def _glu_kernel(y_ref, w_ref, b_ref, o_ref, z_ref):
    jn = pl.program_id(1)
    tn = o_ref.shape[-1]

    @pl.when(jn == 0)
    def _():
        for q in range(y_ref.shape[0]):
            z_ref[:, q * LANES:(q + 1) * LANES] = jax.nn.gelu(y_ref[q].astype(F32)).astype(z_ref.dtype)

    gate = jnp.dot(z_ref[...], w_ref[...], preferred_element_type=F32) + b_ref[...]
    zc = z_ref[:, pl.ds(pl.multiple_of(jn * tn, tn), tn)].astype(F32)
    o_ref[...] = (zc * jax.nn.sigmoid(gate)).astype(o_ref.dtype)


def _glu(y_blk, glu_w, glu_b):
    jb, n, _ = y_blk.shape
    d = jb * LANES
    tm, tn = _tile(n, 1024), _tile(d, 1024)
    return pl.pallas_call(
        _glu_kernel,
        grid=(n // tm, d // tn),
        in_specs=[pl.BlockSpec((jb, tm, LANES), lambda i, j: (0, i, 0)),
                  pl.BlockSpec((d, tn), lambda i, j: (0, j)),
                  pl.BlockSpec((1, tn), lambda i, j: (0, j))],
        out_specs=pl.BlockSpec((tm, tn), lambda i, j: (i, j)),
        out_shape=jax.ShapeDtypeStruct((n, d), BF16),
        scratch_shapes=[pltpu.VMEM((tm, d), BF16)],
        compiler_params=_cparams(("parallel", "arbitrary")),
        name="s5_glu",
    )(y_blk, glu_w, glu_b.reshape(1, d).astype(F32))


def _nt(a, b):
    return lax.dot_general(a, b, (((1,), (1,)), ((), ())), preferred_element_type=F32)


def _split3(x):
    hi = x.astype(BF16)
    r1 = x - hi.astype(F32)
    mid = r1.astype(BF16)
    lo = (r1 - mid.astype(F32)).astype(BF16)
    return jnp.concatenate([hi, mid, lo], axis=0)


def _gates(z, lbd):
    f = lbd + (1.0 - lbd) * jax.nn.sigmoid(z)
    return 1.0 - f, jnp.log(f)


def _gla_kernel(q_ref, v_ref, zf_ref, zb_ref, go_ref, vm_ref, zm_ref, lb_ref, nw_ref, o_ref,
                acc_ref, vt_ref, k_ref, g3_ref, b_ref, qt_ref, kt_ref, qe_ref, kh_ref, u_ref, dl_ref, st_ref, s_ref):
    L, K = q_ref.shape
    C = GLA_CHUNK
    TR = GLA_TILE
    cpt = TR // C
    n_chunks = L // C
    n_tiles = L // TR
    lb = lb_ref[...]
    shift_c, shift_k = int(math.log2(C)), int(math.log2(K))
    row = lax.broadcasted_iota(jnp.int32, (TR, TR), 0)
    col = lax.broadcasted_iota(jnp.int32, (TR, TR), 1)
    same_chunk = (row >> shift_c) == (col >> shift_c)
    own_block = ((lax.broadcasted_iota(jnp.int32, (TR, cpt * K), 0) >> shift_c)
                 == (lax.broadcasted_iota(jnp.int32, (TR, cpt * K), 1) >> shift_k))

    def crow(c):
        return pl.ds(pl.multiple_of(c * C, C), C)

    def trow(t):
        return pl.ds(pl.multiple_of(t * TR, TR), TR)

    def block_diag(x):
        return jnp.where(own_block, jnp.concatenate([x] * cpt, axis=1), jnp.zeros((), x.dtype))

    def transpose_v(t, _):
        vt_ref[t] = v_ref[trow(t), :].astype(F32).T.astype(BF16)
        return 0

    lax.fori_loop(0, n_tiles, transpose_v, 0, unroll=2)

    m = zm_ref.shape[0]
    km, gm = _gates(zm_ref[...], lb[0:1, :])
    mrow = lax.broadcasted_iota(jnp.int32, (m, m), 0)
    mcol = lax.broadcasted_iota(jnp.int32, (m, m), 1)
    bm = jnp.dot(jnp.concatenate([(mrow >= mcol).astype(BF16)] * 3, axis=1), _split3(gm),
                 preferred_element_type=F32)
    khm = (km * jnp.exp(bm[m - 1:m, :] - bm)).astype(BF16)
    st0 = jnp.dot(vm_ref[...].astype(F32).T.astype(BF16), khm, preferred_element_type=F32)

    for d, z_ref in enumerate((zf_ref, zb_ref)):
        lbd = lb[d:d + 1, :]
        causal = (row >= col) if d == 0 else (col >= row)
        keep = same_chunk & causal
        cum = jnp.concatenate([causal[:C, :C].astype(BF16)] * 3, axis=1)
        ref_row = C // 2 - 1 if d == 0 else C // 2
        edge_row = C - 1 if d == 0 else 0

        def gate_stage(t, _):
            k, g = _gates(z_ref[trow(t), :], lbd)
            k_ref[trow(t), :] = k
            hi = g.astype(BF16)
            r1 = g - hi.astype(F32)
            mid = r1.astype(BF16)
            lo = (r1 - mid.astype(F32)).astype(BF16)
            for part, val in enumerate((hi, mid, lo)):
                for j in range(cpt):
                    g3_ref[t, part * C:(part + 1) * C, j * K:(j + 1) * K] = val[j * C:(j + 1) * C, :]
            return 0

        lax.fori_loop(0, n_tiles, gate_stage, 0, unroll=2)

        def cumsum_stage(t, _):
            bw = jnp.dot(cum, g3_ref[t], preferred_element_type=F32)
            for j in range(cpt):
                b_ref[crow(t * cpt + j), :] = bw[:, j * K:(j + 1) * K]
            return 0

        lax.fori_loop(0, n_tiles, cumsum_stage, 0, unroll=4)

        def scale_stage(c, _):
            rows = crow(c)
            b = b_ref[rows, :]
            r = b[ref_row:ref_row + 1, :]
            e = b[edge_row:edge_row + 1, :]
            qt = q_ref[rows, :].astype(F32) * jnp.exp(b - r)
            kt = k_ref[rows, :] * jnp.exp(r - b)
            qt_ref[rows, :] = qt.astype(BF16)
            kt_ref[rows, :] = kt.astype(BF16)
            qe_ref[rows, :] = (qt * jnp.exp(r)).astype(BF16)
            kh_ref[rows, :] = (kt * jnp.exp(e - r)).astype(BF16)
            dl_ref[pl.ds(c, 1), :] = jnp.exp(e)
            return 0

        lax.fori_loop(0, n_chunks, scale_stage, 0, unroll=2)

        def score_stage(t, _):
            rows = trow(t)
            s_ref[rows, :] = jnp.where(keep, _nt(qt_ref[rows, :], kt_ref[rows, :]), 0.0).astype(BF16)
            return 0

        lax.fori_loop(0, n_tiles, score_stage, 0, unroll=4)

        def intra_stage(t, _):
            rows = trow(t)
            acc_ref[d, rows, :] = jnp.dot(s_ref[rows, :], v_ref[rows, :], preferred_element_type=F32)
            return 0

        lax.fori_loop(0, n_tiles, intra_stage, 0, unroll=4)

        def incr_stage(t, _):
            u_all = jnp.dot(vt_ref[t], block_diag(kh_ref[trow(t), :]), preferred_element_type=F32)
            for j in range(cpt):
                u_ref[t * cpt + j] = u_all[:, j * K:(j + 1) * K]
            return 0

        lax.fori_loop(0, n_tiles, incr_stage, 0, unroll=4)

        def state_stage(i, st):
            c = i if d == 0 else n_chunks - 1 - i
            st_ref[c] = st.astype(BF16)
            return st * dl_ref[pl.ds(c, 1), :] + u_ref[c]

        lax.fori_loop(0, n_chunks, state_stage, st0 if d == 0 else jnp.zeros((K, K), F32), unroll=4)

        def inter_stage(t, _):
            rows = trow(t)
            st_cat = jnp.concatenate([st_ref[t * cpt + j] for j in range(cpt)], axis=1)
            acc_ref[d, rows, :] += _nt(block_diag(qe_ref[rows, :]), st_cat)
            return 0

        lax.fori_loop(0, n_tiles, inter_stage, 0, unroll=2)

    o = acc_ref[0] + acc_ref[1]
    o = o * lax.rsqrt(jnp.mean(o * o, axis=-1, keepdims=True) + RMS_EPS)
    o_ref[...] = (o * nw_ref[...] * go_ref[...].astype(F32)).astype(o_ref.dtype)


def _gla(q, v, z_fb, go, v_meta, zf_meta, lb, norm_w, n_batch):
    n, d = q.shape
    L = n // n_batch
    heads = d // HG_EXPAND
    m = v_meta.shape[0]
    C = GLA_CHUNK
    n_chunks = L // C
    hk = HG_EXPAND
    tok = pl.BlockSpec((L, hk), lambda b, h: (b, h))
    tok_bwd = pl.BlockSpec((L, hk), lambda b, h: (b, h + heads))
    return pl.pallas_call(
        _gla_kernel,
        grid=(n_batch, heads),
        in_specs=[tok, tok, tok, tok_bwd, tok,
                  pl.BlockSpec((m, hk), lambda b, h: (0, h)),
                  pl.BlockSpec((m, hk), lambda b, h: (0, h)),
                  pl.BlockSpec((2, hk), lambda b, h: (0, h)),
                  pl.BlockSpec((1, hk), lambda b, h: (0, h))],
        out_specs=tok,
        out_shape=jax.ShapeDtypeStruct((n, d), BF16),
        scratch_shapes=[pltpu.VMEM((2, L, hk), F32),
                        pltpu.VMEM((L // GLA_TILE, hk, GLA_TILE), BF16),
                        pltpu.VMEM((L, hk), F32),
                        pltpu.VMEM((L // GLA_TILE, 3 * C, (GLA_TILE // C) * hk), BF16),
                        pltpu.VMEM((L, hk), F32),
                        pltpu.VMEM((L, hk), BF16),
                        pltpu.VMEM((L, hk), BF16),
                        pltpu.VMEM((L, hk), BF16),
                        pltpu.VMEM((L, hk), BF16),
                        pltpu.VMEM((n_chunks, hk, hk), F32),
                        pltpu.VMEM((n_chunks, hk), F32),
                        pltpu.VMEM((n_chunks, hk, hk), BF16),
                        pltpu.VMEM((L, GLA_TILE), BF16)],
        compiler_params=_cparams(("parallel", "parallel")),
        name="hgrn2_gla",
    )(q, v, z_fb, z_fb, go, v_meta, zf_meta, lb, norm_w.reshape(1, d).astype(F32))


def _merge_kernel(a1_ref, w1_ref, a2_ref, w2_ref, g1_ref, g2_ref, o_ref):
    y1 = jnp.dot(a1_ref[...], w1_ref[...], preferred_element_type=F32)
    y2 = jnp.dot(a2_ref[...], w2_ref[...], preferred_element_type=F32)
    o_ref[...] = (g1_ref[...].astype(F32) * y1 + g2_ref[...].astype(F32) * y2).astype(o_ref.dtype)


def _merge(a1, w1, a2, w2, gates):
    n, k1 = a1.shape
    k2 = a2.shape[1]
    d = w1.shape[1]
    tm, tn = _tile(n, 1024), _tile(d, 1024)
    nj = d // tn
    o_spec = pl.BlockSpec((tm, tn), lambda i, j: (i, j))
    return pl.pallas_call(
        _merge_kernel,
        grid=(n // tm, nj),
        in_specs=[pl.BlockSpec((tm, k1), lambda i, j: (i, 0)), pl.BlockSpec((k1, tn), lambda i, j: (0, j)),
                  pl.BlockSpec((tm, k2), lambda i, j: (i, 0)), pl.BlockSpec((k2, tn), lambda i, j: (0, j)),
                  o_spec, pl.BlockSpec((tm, tn), lambda i, j: (i, j + nj))],
        out_specs=o_spec,
        out_shape=jax.ShapeDtypeStruct((n, d), BF16),
        compiler_params=_cparams(("parallel", "parallel")),
        name="branch_merge",
    )(a1, w1, a2, w2, gates, gates)


def _pack_bf16_pair(lo, hi):
    lo_bits = pltpu.bitcast(lo.astype(BF16).astype(F32), jnp.int32)
    hi_bits = pltpu.bitcast(hi.astype(BF16).astype(F32), jnp.int32)
    return lax.shift_right_logical(lo_bits, 16) | hi_bits


def _unpack_bf16_pair(w):
    lo = pltpu.bitcast(w << 16, F32)
    hi = pltpu.bitcast(w & jnp.int32(-65536), F32)
    return jnp.concatenate([lo, hi], axis=-1)


def _router_kernel(h_ref, nw_ref, rw_ref, rb_ref, f_ref, rt_ref, *, n_groups, per_group):
    x = h_ref[...]
    f = x * lax.rsqrt(jnp.mean(x * x, axis=-1, keepdims=True) + RMS_EPS) * nw_ref[...]
    half = f.shape[1] // 2
    f_ref[...] = _pack_bf16_pair(f[:, :half], f[:, half:])
    logits = jnp.dot(f, rw_ref[...], precision=lax.Precision.HIGHEST, preferred_element_type=F32) + rb_ref[...]
    lane = lax.broadcasted_iota(jnp.int32, logits.shape, 1)
    lane_f = lane.astype(F32)
    big = float(logits.shape[1])
    neg = -jnp.inf

    def first_argmax(v):
        vmax = jnp.max(v, axis=-1, keepdims=True)
        return vmax, jnp.min(jnp.where(v == vmax, lane_f, big), axis=-1, keepdims=True)

    lg = jnp.where(lane < n_groups, logits, neg)
    g_max, g_sel = first_argmax(lg)
    g_prob = 1.0 / jnp.sum(jnp.exp(lg - g_max), axis=-1, keepdims=True)
    lo = n_groups + g_sel * per_group
    le = jnp.where((lane_f >= lo) & (lane_f < lo + per_group), logits, neg)
    v1, i1 = first_argmax(le)
    v2, i2 = first_argmax(jnp.where(lane_f == i1, neg, le))
    t = jnp.exp(v2 - v1)
    w1 = g_prob / (1.0 + t)
    w2 = g_prob * t / (1.0 + t)
    rt = jnp.where(lane == 0, i1 - n_groups, jnp.where(lane == 1, i2 - n_groups,
                   jnp.where(lane == 2, w1, jnp.where(lane == 3, w2, 0.0))))
    rt_ref[...] = rt


def _router(h, norm_w, rw, rb, n_groups, per_group):
    n, d = h.shape
    tm = _tile(n, 256)
    return pl.pallas_call(
        functools.partial(_router_kernel, n_groups=n_groups, per_group=per_group),
        grid=(n // tm,),
        in_specs=[pl.BlockSpec((tm, d), lambda i: (i, 0)), pl.BlockSpec((1, d), lambda i: (0, 0)),
                  pl.BlockSpec((d, ROUTER_COLS), lambda i: (0, 0)),
                  pl.BlockSpec((1, ROUTER_COLS), lambda i: (0, 0))],
        out_specs=[pl.BlockSpec((tm, d // 2), lambda i: (i, 0)), pl.BlockSpec((tm, ROUTER_COLS), lambda i: (i, 0))],
        out_shape=[jax.ShapeDtypeStruct((n, d // 2), jnp.int32), jax.ShapeDtypeStruct((n, ROUTER_COLS), F32)],
        compiler_params=_cparams(("parallel",)),
        name="ffn_norm_router",
    )(h, norm_w.reshape(1, d).astype(F32), rw, rb)


def _row_copy(src_hbm, dst_vmem, sem, src_row, slot, dst_row):
    return pltpu.make_async_copy(src_hbm.at[pl.ds(src_row, 1)], dst_vmem.at[slot, pl.ds(dst_row, 1)], sem.at[slot])


def _moe_kernel(blk_e_ref, n_used_ref, tok_ref, f_hbm, w1_ref, w3_ref, w2_ref, gw_ref, o_ref, xs_ref, sem):
    del blk_e_ref
    i = pl.program_id(0)
    n_used = n_used_ref[0]
    rows = xs_ref.shape[1]

    def start(blk, slot):
        def body(r, _):
            _row_copy(f_hbm, xs_ref, sem, tok_ref[blk * rows + r], slot, r).start()
            return 0
        lax.fori_loop(0, rows, body, 0, unroll=8)

    def wait(slot):
        def body(r, _):
            _row_copy(f_hbm, xs_ref, sem, 0, slot, r).wait()
            return 0
        lax.fori_loop(0, rows, body, 0, unroll=8)

    @pl.when(jnp.logical_and(i == 0, n_used > 0))
    def _():
        start(0, 0)

    @pl.when(i + 1 < n_used)
    def _():
        start(i + 1, (i + 1) % 2)

    @pl.when(i < n_used)
    def _():
        slot = i % 2
        wait(slot)
        x = _unpack_bf16_pair(xs_ref[slot]).astype(BF16)
        h1 = jnp.dot(x, w1_ref[...].astype(BF16), preferred_element_type=F32)
        h3 = jnp.dot(x, w3_ref[...].astype(BF16), preferred_element_type=F32)
        hb = (jax.nn.silu(h1) * h3).astype(BF16)
        y = jnp.dot(hb, w2_ref[...].astype(BF16), preferred_element_type=F32) * gw_ref[...]
        half = y.shape[1] // 2
        o_ref[...] = _pack_bf16_pair(y[:, :half], y[:, half:])

    @pl.when(i >= n_used)
    def _():
        o_ref[...] = jnp.zeros_like(o_ref)


def _moe(f, w1, w3, w2, blk_e, n_used, buf_tok, buf_w):
    n = f.shape[0]
    e, d, hdim = w1.shape
    dh = d // 2
    p = buf_tok.shape[0]
    n_blocks = p // MOE_BLOCK
    single = pl.Buffered(1)
    grid_spec = pltpu.PrefetchScalarGridSpec(
        num_scalar_prefetch=3,
        grid=(n_blocks,),
        in_specs=[pl.BlockSpec(memory_space=pl.ANY),
                  pl.BlockSpec((None, d, hdim), lambda i, be, nu, tk: (be[i], 0, 0)),
                  pl.BlockSpec((None, d, hdim), lambda i, be, nu, tk: (be[i], 0, 0)),
                  pl.BlockSpec((None, hdim, d), lambda i, be, nu, tk: (be[i], 0, 0), pipeline_mode=single),
                  pl.BlockSpec((MOE_BLOCK, 1), lambda i, be, nu, tk: (i, 0))],
        out_specs=pl.BlockSpec((MOE_BLOCK, dh), lambda i, be, nu, tk: (i, 0)),
        scratch_shapes=[pltpu.VMEM((2, MOE_BLOCK, dh), jnp.int32), pltpu.SemaphoreType.DMA((2,))],
    )
    return pl.pallas_call(
        _moe_kernel,
        grid_spec=grid_spec,
        out_shape=jax.ShapeDtypeStruct((p, dh), jnp.int32),
        compiler_params=_cparams(("arbitrary",)),
        name="moe_experts",
    )(blk_e, n_used, buf_tok, f, w1, w3, w2, buf_w.reshape(p, 1))


def _combine_kernel(pos_ref, h_ref, ys_hbm, nw_ref, o_ref, yg_ref, sem):
    i = pl.program_id(0)
    n_steps = pl.num_programs(0)
    tm = h_ref.shape[0]
    rows = yg_ref.shape[1]

    def start(step, slot):
        def body(r, _):
            _row_copy(ys_hbm, yg_ref, sem, pos_ref[step * rows + r], slot, r).start()
            return 0
        lax.fori_loop(0, rows, body, 0, unroll=8)

    def wait(slot):
        def body(r, _):
            _row_copy(ys_hbm, yg_ref, sem, 0, slot, r).wait()
            return 0
        lax.fori_loop(0, rows, body, 0, unroll=8)

    @pl.when(i == 0)
    def _():
        start(0, 0)

    @pl.when(i + 1 < n_steps)
    def _():
        start(i + 1, (i + 1) % 2)

    slot = i % 2
    wait(slot)
    h = h_ref[...]
    for k in range(TOP_K_INNER):
        h = h + _unpack_bf16_pair(yg_ref[slot, k * tm:(k + 1) * tm, :])
    y = h * lax.rsqrt(jnp.mean(h * h, axis=-1, keepdims=True) + RMS_EPS)
    o_ref[...] = y * nw_ref[...]


def _combine(h, ys, pos, norm_w):
    n, d = h.shape
    tm = _tile(n, 128)
    grid_spec = pltpu.PrefetchScalarGridSpec(
        num_scalar_prefetch=1,
        grid=(n // tm,),
        in_specs=[pl.BlockSpec((tm, d), lambda i, ps: (i, 0)),
                  pl.BlockSpec(memory_space=pl.ANY),
                  pl.BlockSpec((1, d), lambda i, ps: (0, 0))],
        out_specs=pl.BlockSpec((tm, d), lambda i, ps: (i, 0)),
        scratch_shapes=[pltpu.VMEM((2, TOP_K_INNER * tm, d // 2), jnp.int32), pltpu.SemaphoreType.DMA((2,))],
    )
    return pl.pallas_call(
        _combine_kernel,
        grid_spec=grid_spec,
        out_shape=jax.ShapeDtypeStruct((n, d), F32),
        compiler_params=_cparams(("arbitrary",)),
        name="moe_combine_final_norm",
    )(pos, h, ys, norm_w.reshape(1, d).astype(F32))


def _routing(route, n_tok, tm_combine):
    kk = TOP_K_INNER
    n_exp = N_EXPERT_GROUPS * EXPERTS_PER_GROUP
    blk = MOE_BLOCK
    m = n_tok * kk
    flat_e = route[:, :kk].astype(jnp.int32).reshape(m)
    flat_w = route[:, kk:2 * kk].reshape(m)
    iota_m = jnp.arange(m, dtype=jnp.int32)
    se, order, sw = lax.sort((flat_e, iota_m, flat_w), num_keys=1, is_stable=True)
    counts = jnp.sum((flat_e[:, None] == jnp.arange(n_exp, dtype=jnp.int32)[None, :]).astype(jnp.int32), axis=0)
    padded = ((counts + blk - 1) // blk) * blk
    pad_end = jnp.cumsum(padded)
    pad_start = pad_end - padded
    grp_start = jnp.cumsum(counts) - counts
    n_blocks = -(-m // blk) + n_exp
    p = n_blocks * blk
    blk_e = jnp.minimum(jnp.searchsorted(pad_end, jnp.arange(n_blocks, dtype=jnp.int32) * blk, side='right'),
                        n_exp - 1).astype(jnp.int32)
    n_used = (pad_end[-1] // blk).astype(jnp.int32).reshape(1)
    off = jnp.arange(p, dtype=jnp.int32).reshape(n_blocks, blk) - pad_start[blk_e][:, None]
    valid = off < counts[blk_e][:, None]
    src = jnp.clip(grp_start[blk_e][:, None] + off, 0, m - 1)
    buf_tok = jnp.where(valid, order[src] // kk, 0).reshape(p).astype(jnp.int32)
    buf_w = jnp.where(valid, sw[src], 0.0).reshape(p)
    dest = (pad_start[se] + iota_m - grp_start[se]).astype(jnp.int32)
    _, pos_flat = lax.sort((order, dest), num_keys=1)
    pos = pos_flat.reshape(n_tok // tm_combine, tm_combine, kk).transpose(0, 2, 1).reshape(m)
    return blk_e, n_used, buf_tok, buf_w, pos


def kernel(x, meta_tokens, norm_mix_w, w_in, s5_a_re, s5_a_im, s5_log_dt, s5_b_re, s5_b_im, s5_c_re, s5_c_im, s5_d, s5_glu_w, s5_glu_b, hg_lb_gamma, hg_norm_w, w_branch_a, w_branch_b, w_out, norm_ffn_w, router_group_w, router_group_b, router_expert_w, router_expert_b, expert_w1, expert_w3, expert_w2, norm_final_w):
    n_batch, seq, d_model = x.shape
    depth = w_in.shape[0]
    n_tok = n_batch * seq
    d_s5 = s5_d.shape[-1]
    d_hg = hg_norm_w.shape[-1]
    T = S5_CHUNK
    assert meta_tokens.shape[0] == N_META == T and seq % GLA_TILE == 0
    jb = d_s5 // LANES
    gb = LANES // S5_GROUP
    n_state = s5_a_re.shape[-1]
    sig = jax.nn.sigmoid
    o_u, o_q, o_i, o_f, o_g, o_ga = 0, d_s5, d_s5 + d_hg, d_s5 + 2 * d_hg, d_s5 + 4 * d_hg, d_s5 + 5 * d_hg

    h = x.reshape(n_tok, d_model)
    h_meta = meta_tokens.astype(x.dtype)
    for l in range(depth):
        assert l == 0, "the meta-token shortcut is only valid for a single layer"
        a = _rmsnorm(h, norm_mix_w[l])
        a_m = _rmsnorm(h_meta, norm_mix_w[l])
        w = w_in[l].astype(BF16)
        proj = functools.partial(_matmul, [(a, w)])
        proj_m = functools.partial(_matmul, [(a_m, w)], out_dtype=F32)

        u_blk = proj(n_cols=d_s5, col_off=o_u, lane_blocked=True, name="proj_u")
        q = proj(n_cols=d_hg, col_off=o_q, epilogue=jax.nn.silu, name="proj_q")
        v = proj(n_cols=d_hg, col_off=o_i, name="proj_i")
        z_fb = proj(n_cols=2 * d_hg, col_off=o_f, out_dtype=F32, name="proj_f")
        g_out = proj(n_cols=d_hg, col_off=o_g, epilogue=jax.nn.silu, name="proj_g")
        gates = proj(n_cols=2 * d_model, col_off=o_ga, epilogue=sig, name="proj_gates")
        u_m = proj_m(n_cols=d_s5, col_off=o_u, name="proj_u_meta")
        v_m = proj_m(n_cols=d_hg, col_off=o_i, out_dtype=BF16, name="proj_i_meta")
        zf_m = proj_m(n_cols=d_hg, col_off=o_f, name="proj_f_meta")

        ec, mc, fc, lam_t = _s5_compact_operators(
            s5_a_re[l].astype(F32), s5_a_im[l].astype(F32), s5_log_dt[l].astype(F32), s5_b_re[l].astype(F32),
            s5_b_im[l].astype(F32), s5_c_re[l].astype(F32), s5_c_im[l].astype(F32), s5_d[l].astype(F32))
        rep_tq = _replicator(T, S5_GROUP, gb)
        rep_dp = _replicator(4, n_state, gb)
        e_mat = _expand(ec, rep_dp, S5_GROUP, n_state, gb, "s5_expand_e")
        m_mat = _expand(mc, rep_tq, S5_GROUP, S5_GROUP, gb, "s5_expand_m")
        f_mat = _expand(fc, rep_tq, n_state, S5_GROUP, gb, "s5_expand_f")
        sw = e_mat.shape[-1]
        u_ch = u_blk.reshape(jb, n_tok // T, T * LANES)
        um = u_m.reshape(T, jb, LANES).transpose(1, 0, 2).reshape(jb, 1, T * LANES)
        um = jnp.pad(um, ((0, 0), (0, 7), (0, 0))).astype(BF16)
        x0 = _matmul([(um, e_mat)], n_cols=sw, out_dtype=F32, name="s5_state_meta")[:, 0:1, :]
        x0 = jnp.concatenate([x0[..., :sw // 2], jnp.zeros_like(x0[..., sw // 2:])], axis=-1)
        s_all = _matmul([(u_ch, e_mat)], n_cols=sw, out_dtype=F32, name="s5_chunk_state")
        x_prev = _s5_scan(s_all, lam_t, x0, n_batch)
        y_blk = _matmul([(u_ch, m_mat), (x_prev, f_mat)], n_cols=T * LANES, name="s5_chunk_out")
        s5_out = _glu(y_blk.reshape(jb, n_tok, LANES), s5_glu_w[l].astype(BF16), s5_glu_b[l])

        lb = jnp.cumsum(jax.nn.softmax(hg_lb_gamma.astype(F32), axis=1), axis=1)[:, l]
        hg_out = _gla(q, v, z_fb, g_out, v_m, zf_m, lb, hg_norm_w[l], n_batch)

        mrg = _merge(s5_out, w_branch_a[l].astype(BF16), hg_out, w_branch_b[l].astype(BF16), gates)
        h = _matmul([(mrg, w_out[l].astype(BF16))], n_cols=d_model, extras=(h,),
                    epilogue=lambda acc, r: acc + r, out_dtype=F32, name="out_proj")

        ng, n_exp = router_group_w.shape[-1], router_expert_w.shape[-1]
        rw = jnp.concatenate([router_group_w[l], router_expert_w[l],
                              jnp.zeros((d_model, ROUTER_COLS - ng - n_exp), F32)], axis=1).astype(F32)
        rb = jnp.concatenate([router_group_b[l], router_expert_b[l],
                              jnp.zeros((ROUTER_COLS - ng - n_exp,), F32)]).reshape(1, ROUTER_COLS).astype(F32)
        assert (ng, n_exp) == (N_EXPERT_GROUPS, N_EXPERT_GROUPS * EXPERTS_PER_GROUP)
        f, route = _router(h, norm_ffn_w[l], rw, rb, ng, n_exp // ng)
        tm_c = _tile(n_tok, 128)
        blk_e, n_used, buf_tok, buf_w, pos = _routing(route, n_tok, tm_c)
        ys = _moe(f, expert_w1[l], expert_w3[l], expert_w2[l], blk_e, n_used, buf_tok, buf_w)
    out = _combine(h, ys, pos, norm_final_w)
    return out.reshape(n_batch, seq, d_model)
```

```python
import functools
import math

import jax
import jax.numpy as jnp
from jax import lax
from jax.experimental import pallas as pl
from jax.experimental.pallas import tpu as pltpu

F32 = jnp.float32
BF16 = jnp.bfloat16

N_META = 16
S5_GROUP = 16
S5_STATE = 64
S5_CHUNK = 16
HG_EXPAND = 128
GLA_CHUNK = 64
GLA_TILE = 256
GLA_SAFE_SPAN = 60.0
N_EXPERT_GROUPS = 8
EXPERTS_PER_GROUP = 8
TOP_K_INNER = 2
MOE_BLOCK = 256
RMS_EPS = 1e-6
LANES = 128
ROUTER_COLS = 128
VMEM_LIMIT = 56 * 1024 * 1024


def _cparams(sem):
    return pltpu.CompilerParams(dimension_semantics=sem, vmem_limit_bytes=VMEM_LIMIT)


def _tile(n, pref):
    t = min(n, pref)
    assert n % t == 0, (n, pref)
    return t


def _rmsnorm_kernel(x_ref, w_ref, o_ref):
    x = x_ref[...]
    y = x * lax.rsqrt(jnp.mean(x * x, axis=-1, keepdims=True) + RMS_EPS)
    o_ref[...] = (y * w_ref[...]).astype(o_ref.dtype)


def _rmsnorm(x, w, out_dtype=BF16):
    n, d = x.shape
    tm = _tile(n, 256)
    return pl.pallas_call(
        _rmsnorm_kernel,
        grid=(n // tm,),
        in_specs=[pl.BlockSpec((tm, d), lambda i: (i, 0)), pl.BlockSpec((1, d), lambda i: (0, 0))],
        out_specs=pl.BlockSpec((tm, d), lambda i: (i, 0)),
        out_shape=jax.ShapeDtypeStruct((n, d), out_dtype),
        compiler_params=_cparams(("parallel",)),
        name="rmsnorm",
    )(x, w.reshape(1, d).astype(F32))


def _mm_kernel(*refs, n_pairs, n_extra, epilogue, lane_blocked):
    a_refs = refs[0:2 * n_pairs:2]
    w_refs = refs[1:2 * n_pairs:2]
    extras = refs[2 * n_pairs:2 * n_pairs + n_extra]
    o_ref = refs[2 * n_pairs + n_extra]
    acc = None
    for a_ref, w_ref in zip(a_refs, w_refs):
        d = jnp.dot(a_ref[...], w_ref[...], preferred_element_type=F32)
        acc = d if acc is None else acc + d
    res = epilogue(acc, *[e[...] for e in extras]).astype(o_ref.dtype)
    if lane_blocked:
        for q in range(o_ref.shape[0]):
            o_ref[q] = res[:, q * LANES:(q + 1) * LANES]
    else:
        o_ref[...] = res


def _matmul(pairs, *, n_cols, col_off=0, extras=(), epilogue=None, out_dtype=BF16, lane_blocked=False,
            tm=1024, tn=1024, name="matmul"):
    if epilogue is None:
        epilogue = lambda acc: acc
    a0 = pairs[0][0]
    batched = a0.ndim == 3
    g = a0.shape[0] if batched else 1
    m = a0.shape[-2]
    tm = _tile(m, tm)
    tn = _tile(math.gcd(n_cols, col_off) if col_off else n_cols, tn)
    cb = col_off // tn
    lead = (None,) if batched else ()

    def bidx(b):
        return (b,) if batched else ()

    in_specs, args = [], []
    for a, w in pairs:
        kdim = a.shape[-1]
        assert a.dtype == BF16 and w.dtype == BF16 and w.shape[-2] == kdim and a.shape[-2] == m
        in_specs.append(pl.BlockSpec(lead + (tm, kdim), lambda b, i, j: bidx(b) + (i, 0)))
        in_specs.append(pl.BlockSpec(lead + (kdim, tn), lambda b, i, j: bidx(b) + (0, j + cb)))
        args += [a, w]
    for e in extras:
        if e.shape[-2] == 1:
            in_specs.append(pl.BlockSpec((1, tn), lambda b, i, j: (0, j)))
        else:
            in_specs.append(pl.BlockSpec((tm, tn), lambda b, i, j: (i, j)))
        args.append(e)
    if lane_blocked:
        assert not batched
        out_shape = jax.ShapeDtypeStruct((n_cols // LANES, m, LANES), out_dtype)
        out_spec = pl.BlockSpec((tn // LANES, tm, LANES), lambda b, i, j: (j, i, 0))
    else:
        out_shape = jax.ShapeDtypeStruct(((g,) if batched else ()) + (m, n_cols), out_dtype)
        out_spec = pl.BlockSpec(lead + (tm, tn), lambda b, i, j: bidx(b) + (i, j))
    return pl.pallas_call(
        functools.partial(_mm_kernel, n_pairs=len(pairs), n_extra=len(extras), epilogue=epilogue,
                          lane_blocked=lane_blocked),
        grid=(g, m // tm, n_cols // tn),
        in_specs=in_specs,
        out_specs=out_spec,
        out_shape=out_shape,
        compiler_params=_cparams(("parallel", "parallel", "parallel")),
        name=name,
    )(*args)


def _s5_compact_operators(a_re, a_im, log_dt, b_re, b_im, c_re, c_im, d_skip):
    T = S5_CHUNK
    hp = lax.Precision.HIGHEST
    G, P = a_re.shape[1:]
    H = b_re.shape[-1]
    GB = LANES // H
    J = G // GB
    dt = jnp.exp(log_dt)[..., None]
    lre, lang = a_re * dt, a_im * dt
    mag = jnp.exp(lre)
    ab_re, ab_im = mag * jnp.cos(lang), mag * jnp.sin(lang)
    den = a_re * a_re + a_im * a_im
    nr, ni = ab_re - 1.0, ab_im
    z_re = ((nr * a_re + ni * a_im) / den)[..., None]
    z_im = ((ni * a_re - nr * a_im) / den)[..., None]
    bb_re = z_re * b_re - z_im * b_im
    bb_im = z_re * b_im + z_im * b_re
    bbt_re, bbt_im = bb_re.transpose(0, 1, 3, 2), bb_im.transpose(0, 1, 3, 2)
    n = jnp.arange(T + 1, dtype=F32)[:, None]
    pmag = jnp.exp(lre[:, :, None, :] * n)
    pw_re = pmag * jnp.cos(lang[:, :, None, :] * n)
    pw_im = pmag * jnp.sin(lang[:, :, None, :] * n)
    cl_re = c_re[:, :, None] * pw_re[:, :, :, None, :] - c_im[:, :, None] * pw_im[:, :, :, None, :]
    cl_im = c_re[:, :, None] * pw_im[:, :, :, None, :] + c_im[:, :, None] * pw_re[:, :, :, None, :]
    lhs = jnp.concatenate([bbt_re, -bbt_im], axis=-1)
    rhs = jnp.concatenate([cl_re, cl_im], axis=-1).reshape(2, G, (T + 1) * H, 2 * P)
    kk = jnp.einsum('dghk,dgmk->dghm', lhs, rhs, precision=hp).reshape(2, G, H, T + 1, H)
    eye_h = jnp.eye(H, dtype=F32)
    kk_f = kk[0].at[:, :, 0, :].add(eye_h[None] * d_skip.reshape(G, H)[:, :, None])
    kf = kk_f[:, :, :T].reshape(J, LANES, T * H)
    kb_rev = jnp.flip(kk[1][:, :, :T], axis=2).reshape(J, LANES, T * H)

    ar = jnp.arange(T)
    e_parts = []
    for d, idx in ((0, T - 1 - ar), (1, ar)):
        pr, pi = pw_re[d][:, idx, None, :], pw_im[d][:, idx, None, :]
        br, bi = bbt_re[d][:, None], bbt_im[d][:, None]
        e_parts += [pr * br - pi * bi, pr * bi + pi * br]
    ec = jnp.stack(e_parts, axis=3).reshape(J, GB, T, H, 4 * P).transpose(0, 2, 1, 3, 4).reshape(J, T * LANES, 4 * P)

    ct_re, ct_im = c_re.transpose(0, 1, 3, 2), c_im.transpose(0, 1, 3, 2)
    pt_re, pt_im = pw_re.transpose(0, 1, 3, 2), pw_im.transpose(0, 1, 3, 2)
    f_parts = []
    for d, idx in ((0, ar + 1), (1, T - ar)):
        qr, qi = pt_re[d][:, :, idx, None], pt_im[d][:, :, idx, None]
        cr, ci = ct_re[d][:, :, None, :], ct_im[d][:, :, None, :]
        f_parts += [cr * qr - ci * qi, -(cr * qi + ci * qr)]
    fc = jnp.stack(f_parts, axis=0).reshape(4, J, GB * P, T * H).transpose(1, 0, 2, 3).reshape(J, 4 * GB * P, T * H)

    lam_t = jnp.stack([pw_re[0][:, T], pw_im[0][:, T], pw_re[1][:, T], pw_im[1][:, T]], 0)
    lam_t = lam_t.reshape(4, J, GB * P).transpose(1, 0, 2).reshape(J, 1, 4 * GB * P)
    return ec.astype(BF16), (kf, kb_rev), fc.astype(BF16), lam_t


def _replicator(n_outer, n_inner, reps):
    rows = jnp.arange(n_outer * n_inner)
    cols = jnp.arange(n_outer * reps * n_inner)
    same_o = (rows[:, None] // n_inner) == (cols[None, :] // (reps * n_inner))
    same_i = (rows[:, None] % n_inner) == (cols[None, :] % n_inner)
    return (same_o & same_i).astype(BF16)


def _expand_kernel(c_ref, r_ref, o_ref, *, row_group, col_group, n_groups):
    r, tn = o_ref.shape
    rep = jnp.dot(c_ref[...], r_ref[...], preferred_element_type=F32)
    rows = lax.broadcasted_iota(jnp.int32, (r, 1), 0)
    cols = lax.broadcasted_iota(jnp.int32, (1, tn), 1) + pl.program_id(1) * tn
    row_g = (rows >> int(math.log2(row_group))) & (n_groups - 1)
    col_g = (cols >> int(math.log2(col_group))) & (n_groups - 1)
    o_ref[...] = jnp.where(row_g == col_g, rep, 0.0).astype(o_ref.dtype)


def _expand(compact, rmat, row_group, col_group, n_groups, name):
    j, r, c = compact.shape
    n_cols = rmat.shape[1]
    tn = _tile(n_cols, 1024)
    return pl.pallas_call(
        functools.partial(_expand_kernel, row_group=row_group, col_group=col_group, n_groups=n_groups),
        grid=(j, n_cols // tn),
        in_specs=[pl.BlockSpec((None, r, c), lambda i, k: (i, 0, 0)), pl.BlockSpec((c, tn), lambda i, k: (0, k))],
        out_specs=pl.BlockSpec((None, r, tn), lambda i, k: (i, 0, k)),
        out_shape=jax.ShapeDtypeStruct((j, r, n_cols), BF16),
        compiler_params=_cparams(("parallel", "parallel")),
        name=name,
    )(compact, rmat)


def _expand_toeplitz_kernel(kf_ref, kb_ref, r_ref, o_ref, *, n_tok, lag_width, group, n_groups):
    rb, w = kf_ref.shape
    tn = o_ref.shape[-1]
    kf, kb = kf_ref[...], kb_ref[...]
    lane = lax.broadcasted_iota(jnp.int32, (rb, w), 1)
    rows = lax.broadcasted_iota(jnp.int32, (rb, 1), 0)
    cols = lax.broadcasted_iota(jnp.int32, (1, tn), 1) + pl.program_id(1) * tn
    shift = int(math.log2(group))
    keep = ((rows >> shift) & (n_groups - 1)) == ((cols >> shift) & (n_groups - 1))
    for s in range(n_tok):
        lo, hi = s * lag_width, (s + 1) * lag_width
        fwd = jnp.where(lane >= lo, kf if lo == 0 else pltpu.roll(kf, lo, 1), 0.0)
        bwd = jnp.where(lane < hi, kb if hi == w else pltpu.roll(kb, hi, 1), 0.0)
        rep = jnp.dot((fwd + bwd).astype(BF16), r_ref[...], preferred_element_type=F32)
        o_ref[s * rb:(s + 1) * rb, :] = jnp.where(keep, rep, 0.0).astype(o_ref.dtype)


def _expand_toeplitz(kf, kb_rev, rmat, n_tok, group, n_groups, name):
    j, rb, w = kf.shape
    n_cols = rmat.shape[1]
    tn = _tile(n_cols, 1024)
    lag = pl.BlockSpec((None, rb, w), lambda i, k: (i, 0, 0))
    return pl.pallas_call(
        functools.partial(_expand_toeplitz_kernel, n_tok=n_tok, lag_width=w // n_tok, group=group, n_groups=n_groups),
        grid=(j, n_cols // tn),
        in_specs=[lag, lag, pl.BlockSpec((w, tn), lambda i, k: (0, k))],
        out_specs=pl.BlockSpec((None, n_tok * rb, tn), lambda i, k: (i, 0, k)),
        out_shape=jax.ShapeDtypeStruct((j, n_tok * rb, n_cols), BF16),
        compiler_params=_cparams(("parallel", "parallel")),
        name=name,
    )(kf, kb_rev, rmat)


def _s5_scan_kernel(s_ref, lam_ref, x0_ref, o_ref, xp_ref, *, n_batch, n_chunks):
    hw = s_ref.shape[-1] // 4
    lam = lam_ref[...]
    lfr, lfi, lbr, lbi = (lam[:, q * hw:(q + 1) * hw] for q in range(4))
    x0 = x0_ref[...]
    zeros = jnp.zeros((1, hw), F32)
    init = tuple((x0[:, 0:hw], x0[:, hw:2 * hw], zeros, zeros) for _ in range(n_batch))

    def step(c, carry):
        out = []
        for b, (xfr, xfi, xbr, xbi) in enumerate(carry):
            row_f = pl.ds(b * n_chunks + c, 1)
            row_b = pl.ds(b * n_chunks + n_chunks - 1 - c, 1)
            xp_ref[row_f, 0:hw] = xfr
            xp_ref[row_f, hw:2 * hw] = xfi
            xp_ref[row_b, 2 * hw:3 * hw] = xbr
            xp_ref[row_b, 3 * hw:4 * hw] = xbi
            sf = s_ref[row_f, 0:2 * hw]
            sb = s_ref[row_b, 2 * hw:4 * hw]
            out.append((lfr * xfr - lfi * xfi + sf[:, 0:hw],
                        lfr * xfi + lfi * xfr + sf[:, hw:2 * hw],
                        lbr * xbr - lbi * xbi + sb[:, 0:hw],
                        lbr * xbi + lbi * xbr + sb[:, hw:2 * hw]))
        return tuple(out)

    lax.fori_loop(0, n_chunks, step, init)
    o_ref[...] = xp_ref[...].astype(o_ref.dtype)


def _s5_scan(s_all, lam_t, x0, n_batch):
    j, r, w = s_all.shape
    n_chunks = r // n_batch
    return pl.pallas_call(
        functools.partial(_s5_scan_kernel, n_batch=n_batch, n_chunks=n_chunks),
        grid=(j,),
        in_specs=[pl.BlockSpec((None, r, w), lambda i: (i, 0, 0)),
                  pl.BlockSpec((None, 1, w), lambda i: (i, 0, 0)),
                  pl.BlockSpec((None, 1, w), lambda i: (i, 0, 0))],
        out_specs=pl.BlockSpec((None, r, w), lambda i: (i, 0, 0)),
        out_shape=jax.ShapeDtypeStruct((j, r, w), BF16),
        scratch_shapes=[pltpu.VMEM((r, w), F32)],
        compiler_params=_cparams(("parallel",)),
        name="s5_scan",
    )(s_all, lam_t, x0)


def _glu_kernel(y_ref, w_ref, b_ref, o_ref, z_ref):
    jn = pl.program_id(1)
    tn = o_ref.shape[-1]

    @pl.when(jn == 0)
    def _():
        for q in range(y_ref.shape[0]):
            z_ref[:, q * LANES:(q + 1) * LANES] = jax.nn.gelu(y_ref[q].astype(F32)).astype(z_ref.dtype)

    gate = jnp.dot(z_ref[...], w_ref[...], preferred_element_type=F32) + b_ref[...]
    zc = z_ref[:, pl.ds(pl.multiple_of(jn * tn, tn), tn)].astype(F32)
    o_ref[...] = (zc * jax.nn.sigmoid(gate)).astype(o_ref.dtype)


def _glu(y_blk, glu_w, glu_b):
    jb, n, _ = y_blk.shape
    d = jb * LANES
    tm, tn = _tile(n, 1024), _tile(d, 1024)
    return pl.pallas_call(
        _glu_kernel,
        grid=(n // tm, d // tn),
        in_specs=[pl.BlockSpec((jb, tm, LANES), lambda i, j: (0, i, 0)),
                  pl.BlockSpec((d, tn), lambda i, j: (0, j)),
                  pl.BlockSpec((1, tn), lambda i, j: (0, j))],
        out_specs=pl.BlockSpec((tm, tn), lambda i, j: (i, j)),
        out_shape=jax.ShapeDtypeStruct((n, d), BF16),
        scratch_shapes=[pltpu.VMEM((tm, d), BF16)],
        compiler_params=_cparams(("parallel", "arbitrary")),
        name="s5_glu",
    )(y_blk, glu_w, glu_b.reshape(1, d).astype(F32))


def _nt(a, b):
    return lax.dot_general(a, b, (((1,), (1,)), ((), ())), preferred_element_type=F32)


def _split3(x):
    hi = x.astype(BF16)
    r1 = x - hi.astype(F32)
    mid = r1.astype(BF16)
    lo = (r1 - mid.astype(F32)).astype(BF16)
    return jnp.concatenate([hi, mid, lo], axis=0)


def _gates(z, lbd):
    f = lbd + (1.0 - lbd) * jax.nn.sigmoid(z)
    return 1.0 - f, jnp.log(f)


def _gla_kernel(q_ref, v_ref, zf_ref, zb_ref, go_ref, vm_ref, zm_ref, lb_ref, nw_ref, o_ref,
                acc_ref, vt_ref, k_ref, g3_ref, b_ref, qt_ref, kt_ref, qe_ref, kh_ref, u_ref, dl_ref, st_ref, s_ref):
    L, K = q_ref.shape
    C = GLA_CHUNK
    TR = GLA_TILE
    cpt = TR // C
    n_chunks = L // C
    n_tiles = L // TR
    lb = lb_ref[...]
    shift_c, shift_k = int(math.log2(C)), int(math.log2(K))
    row = lax.broadcasted_iota(jnp.int32, (TR, TR), 0)
    col = lax.broadcasted_iota(jnp.int32, (TR, TR), 1)
    same_chunk = (row >> shift_c) == (col >> shift_c)
    own_block = ((lax.broadcasted_iota(jnp.int32, (TR, cpt * K), 0) >> shift_c)
                 == (lax.broadcasted_iota(jnp.int32, (TR, cpt * K), 1) >> shift_k))

    def crow(c):
        return pl.ds(pl.multiple_of(c * C, C), C)

    def trow(t):
        return pl.ds(pl.multiple_of(t * TR, TR), TR)

    def block_diag(x):
        return jnp.where(own_block, jnp.concatenate([x] * cpt, axis=1), jnp.zeros((), x.dtype))

    def transpose_v(t, _):
        vt_ref[t] = v_ref[trow(t), :].astype(F32).T.astype(BF16)
        return 0

    lax.fori_loop(0, n_tiles, transpose_v, 0, unroll=2)

    m = zm_ref.shape[0]
    km, gm = _gates(zm_ref[...], lb[0:1, :])
    mrow = lax.broadcasted_iota(jnp.int32, (m, m), 0)
    mcol = lax.broadcasted_iota(jnp.int32, (m, m), 1)
    bm = jnp.dot(jnp.concatenate([(mrow >= mcol).astype(BF16)] * 3, axis=1), _split3(gm),
                 preferred_element_type=F32)
    khm = (km * jnp.exp(bm[m - 1:m, :] - bm)).astype(BF16)
    st0 = jnp.dot(vm_ref[...].astype(F32).T.astype(BF16), khm, preferred_element_type=F32)

    for d, z_ref in enumerate((zf_ref, zb_ref)):
        lbd = lb[d:d + 1, :]
        causal = (row >= col) if d == 0 else (col >= row)
        keep = same_chunk & causal
        cum = jnp.concatenate([causal[:C, :C].astype(BF16)] * 3, axis=1)
        ref_row = C // 2 - 1 if d == 0 else C // 2
        edge_row = C - 1 if d == 0 else 0

        def gate_stage(t, _):
            k, g = _gates(z_ref[trow(t), :], lbd)
            k_ref[trow(t), :] = k
            hi = g.astype(BF16)
            r1 = g - hi.astype(F32)
            mid = r1.astype(BF16)
            lo = (r1 - mid.astype(F32)).astype(BF16)
            for part, val in enumerate((hi, mid, lo)):
                for j in range(cpt):
                    g3_ref[t, part * C:(part + 1) * C, j * K:(j + 1) * K] = val[j * C:(j + 1) * C, :]
            return 0

        lax.fori_loop(0, n_tiles, gate_stage, 0, unroll=2)

        def cumsum_stage(t, _):
            bw = jnp.dot(cum, g3_ref[t], preferred_element_type=F32)
            for j in range(cpt):
                b_ref[crow(t * cpt + j), :] = bw[:, j * K:(j + 1) * K]
            return 0

        lax.fori_loop(0, n_tiles, cumsum_stage, 0, unroll=4)

        def scale_stage(c, worst):
            rows = crow(c)
            b = b_ref[rows, :]
            r = b[ref_row:ref_row + 1, :]
            e = b[edge_row:edge_row + 1, :]
            q = q_ref[rows, :].astype(F32)
            k = k_ref[rows, :]
            qt_ref[rows, :] = (q * jnp.exp(b - r)).astype(BF16)
            kt_ref[rows, :] = (k * jnp.exp(r - b)).astype(BF16)
            qe_ref[rows, :] = (q * jnp.exp(b)).astype(BF16)
            kh_ref[rows, :] = (k * jnp.exp(e - b)).astype(BF16)
            dl_ref[pl.ds(c, 1), :] = jnp.exp(e)
            span = jnp.maximum(jnp.abs(b[0:1, :] - r), jnp.abs(b[C - 1:C, :] - r))
            return jnp.maximum(worst, span)

        worst = lax.fori_loop(0, n_chunks, scale_stage, jnp.zeros((1, K), F32), unroll=4)

        def score_stage(t, _):
            rows = trow(t)
            s_ref[rows, :] = jnp.where(keep, _nt(qt_ref[rows, :], kt_ref[rows, :]), 0.0).astype(BF16)
            return 0

        lax.fori_loop(0, n_tiles, score_stage, 0, unroll=4)

        @pl.when(jnp.max(worst) > GLA_SAFE_SPAN)
        def _():
            srow = lax.broadcasted_iota(jnp.int32, (C, 1), 0)
            lane_t = lax.broadcasted_iota(jnp.int32, (C, K), 1)

            def exact_tile(ti, _):
                for j in range(cpt):
                    rows = pl.ds(pl.multiple_of(ti * TR + j * C, C), C)
                    b = b_ref[rows, :]
                    qc = q_ref[rows, :].astype(F32)
                    kc = k_ref[rows, :]

                    def token(t, st):
                        pick = srow == t
                        bt = jnp.sum(jnp.where(pick, b, 0.0), axis=0, keepdims=True)
                        qrow = jnp.sum(jnp.where(pick, qc, 0.0), axis=0, keepdims=True)
                        col = jnp.sum(qrow * kc * jnp.exp(jnp.minimum(bt - b, 0.0)), axis=-1, keepdims=True)
                        seen = (srow <= t) if d == 0 else (srow >= t)
                        return jnp.where(lane_t == t, jnp.where(seen, col, 0.0), st)

                    blk = lax.fori_loop(0, C, token, jnp.zeros((C, K), F32)).T[:C, :]
                    zero = jnp.zeros((C, C), F32)
                    s_ref[rows, :] = jnp.concatenate([zero] * j + [blk] + [zero] * (cpt - 1 - j), axis=1).astype(BF16)
                return 0

            lax.fori_loop(0, n_tiles, exact_tile, 0)

        def intra_stage(t, _):
            rows = trow(t)
            acc_ref[d, rows, :] = jnp.dot(s_ref[rows, :], v_ref[rows, :], preferred_element_type=F32)
            return 0

        lax.fori_loop(0, n_tiles, intra_stage, 0, unroll=4)

        def incr_stage(t, _):
            u_all = jnp.dot(vt_ref[t], block_diag(kh_ref[trow(t), :]), preferred_element_type=F32)
            for j in range(cpt):
                u_ref[t * cpt + j] = u_all[:, j * K:(j + 1) * K]
            return 0

        lax.fori_loop(0, n_tiles, incr_stage, 0, unroll=4)

        def state_stage(i, st):
            c = i if d == 0 else n_chunks - 1 - i
            st_ref[c] = st.astype(BF16)
            return st * dl_ref[pl.ds(c, 1), :] + u_ref[c]

        lax.fori_loop(0, n_chunks, state_stage, st0 if d == 0 else jnp.zeros((K, K), F32), unroll=4)

        def inter_stage(t, _):
            rows = trow(t)
            st_cat = jnp.concatenate([st_ref[t * cpt + j] for j in range(cpt)], axis=1)
            acc_ref[d, rows, :] += _nt(block_diag(qe_ref[rows, :]), st_cat)
            return 0

        lax.fori_loop(0, n_tiles, inter_stage, 0, unroll=4)

    o = acc_ref[0] + acc_ref[1]
    o = o * lax.rsqrt(jnp.mean(o * o, axis=-1, keepdims=True) + RMS_EPS)
    o_ref[...] = (o * nw_ref[...] * go_ref[...].astype(F32)).astype(o_ref.dtype)


def _gla(q, v, z_fb, go, v_meta, zf_meta, lb, norm_w, n_batch):
    n, d = q.shape
    L = n // n_batch
    heads = d // HG_EXPAND
    m = v_meta.shape[0]
    C = GLA_CHUNK
    n_chunks = L // C
    hk = HG_EXPAND
    tok = pl.BlockSpec((L, hk), lambda b, h: (b, h))
    tok_bwd = pl.BlockSpec((L, hk), lambda b, h: (b, h + heads))
    return pl.pallas_call(
        _gla_kernel,
        grid=(n_batch, heads),
        in_specs=[tok, tok, tok, tok_bwd, tok,
                  pl.BlockSpec((m, hk), lambda b, h: (0, h)),
                  pl.BlockSpec((m, hk), lambda b, h: (0, h)),
                  pl.BlockSpec((2, hk), lambda b, h: (0, h)),
                  pl.BlockSpec((1, hk), lambda b, h: (0, h))],
        out_specs=tok,
        out_shape=jax.ShapeDtypeStruct((n, d), BF16),
        scratch_shapes=[pltpu.VMEM((2, L, hk), F32),
                        pltpu.VMEM((L // GLA_TILE, hk, GLA_TILE), BF16),
                        pltpu.VMEM((L, hk), F32),
                        pltpu.VMEM((L // GLA_TILE, 3 * C, (GLA_TILE // C) * hk), BF16),
                        pltpu.VMEM((L, hk), F32),
                        pltpu.VMEM((L, hk), BF16),
                        pltpu.VMEM((L, hk), BF16),
                        pltpu.VMEM((L, hk), BF16),
                        pltpu.VMEM((L, hk), BF16),
                        pltpu.VMEM((n_chunks, hk, hk), F32),
                        pltpu.VMEM((n_chunks, hk), F32),
                        pltpu.VMEM((n_chunks, hk, hk), BF16),
                        pltpu.VMEM((L, GLA_TILE), BF16)],
        compiler_params=_cparams(("parallel", "parallel")),
        name="hgrn2_gla",
    )(q, v, z_fb, z_fb, go, v_meta, zf_meta, lb, norm_w.reshape(1, d).astype(F32))


def _merge_kernel(a1_ref, w1_ref, a2_ref, w2_ref, g1_ref, g2_ref, o_ref):
    y1 = jnp.dot(a1_ref[...], w1_ref[...], preferred_element_type=F32)
    y2 = jnp.dot(a2_ref[...], w2_ref[...], preferred_element_type=F32)
    o_ref[...] = (g1_ref[...].astype(F32) * y1 + g2_ref[...].astype(F32) * y2).astype(o_ref.dtype)


def _merge(a1, w1, a2, w2, gates):
    n, k1 = a1.shape
    k2 = a2.shape[1]
    d = w1.shape[1]
    tm, tn = _tile(n, 1024), _tile(d, 1024)
    nj = d // tn
    o_spec = pl.BlockSpec((tm, tn), lambda i, j: (i, j))
    return pl.pallas_call(
        _merge_kernel,
        grid=(n // tm, nj),
        in_specs=[pl.BlockSpec((tm, k1), lambda i, j: (i, 0)), pl.BlockSpec((k1, tn), lambda i, j: (0, j)),
                  pl.BlockSpec((tm, k2), lambda i, j: (i, 0)), pl.BlockSpec((k2, tn), lambda i, j: (0, j)),
                  o_spec, pl.BlockSpec((tm, tn), lambda i, j: (i, j + nj))],
        out_specs=o_spec,
        out_shape=jax.ShapeDtypeStruct((n, d), BF16),
        compiler_params=_cparams(("parallel", "parallel")),
        name="branch_merge",
    )(a1, w1, a2, w2, gates, gates)


def _pack_bf16_pair(lo, hi):
    lo_bits = pltpu.bitcast(lo.astype(BF16).astype(F32), jnp.int32)
    hi_bits = pltpu.bitcast(hi.astype(BF16).astype(F32), jnp.int32)
    return lax.shift_right_logical(lo_bits, 16) | hi_bits


def _unpack_bf16_pair(w):
    lo = pltpu.bitcast(w << 16, F32)
    hi = pltpu.bitcast(w & jnp.int32(-65536), F32)
    return jnp.concatenate([lo, hi], axis=-1)


def _router_kernel(h_ref, nw_ref, rw_ref, rb_ref, f_ref, rt_ref, *, n_groups, per_group):
    x = h_ref[...]
    f = x * lax.rsqrt(jnp.mean(x * x, axis=-1, keepdims=True) + RMS_EPS) * nw_ref[...]
    half = f.shape[1] // 2
    f_ref[...] = _pack_bf16_pair(f[:, :half], f[:, half:])
    f_hi = f.astype(BF16)
    f_lo = (f - f_hi.astype(F32)).astype(BF16)
    logits = jnp.dot(jnp.concatenate([f_hi, f_hi, f_lo], axis=1), rw_ref[...], preferred_element_type=F32) + rb_ref[...]
    lane = lax.broadcasted_iota(jnp.int32, logits.shape, 1)
    lane_f = lane.astype(F32)
    big = float(logits.shape[1])
    neg = -jnp.inf

    def first_argmax(v):
        vmax = jnp.max(v, axis=-1, keepdims=True)
        return vmax, jnp.min(jnp.where(v == vmax, lane_f, big), axis=-1, keepdims=True)

    lg = jnp.where(lane < n_groups, logits, neg)
    g_max, g_sel = first_argmax(lg)
    g_prob = 1.0 / jnp.sum(jnp.exp(lg - g_max), axis=-1, keepdims=True)
    lo = n_groups + g_sel * per_group
    le = jnp.where((lane_f >= lo) & (lane_f < lo + per_group), logits, neg)
    v1, i1 = first_argmax(le)
    v2, i2 = first_argmax(jnp.where(lane_f == i1, neg, le))
    t = jnp.exp(v2 - v1)
    w1 = g_prob / (1.0 + t)
    w2 = g_prob * t / (1.0 + t)
    rt = jnp.where(lane == 0, i1 - n_groups, jnp.where(lane == 1, i2 - n_groups,
                   jnp.where(lane == 2, w1, jnp.where(lane == 3, w2, 0.0))))
    rt_ref[...] = rt


def _router(h, norm_w, rw, rb, n_groups, per_group):
    n, d = h.shape
    tm = _tile(n, 256)
    rw_hi = rw.astype(BF16)
    rw_lo = (rw - rw_hi.astype(F32)).astype(BF16)
    return pl.pallas_call(
        functools.partial(_router_kernel, n_groups=n_groups, per_group=per_group),
        grid=(n // tm,),
        in_specs=[pl.BlockSpec((tm, d), lambda i: (i, 0)), pl.BlockSpec((1, d), lambda i: (0, 0)),
                  pl.BlockSpec((3 * d, ROUTER_COLS), lambda i: (0, 0)),
                  pl.BlockSpec((1, ROUTER_COLS), lambda i: (0, 0))],
        out_specs=[pl.BlockSpec((tm, d // 2), lambda i: (i, 0)), pl.BlockSpec((tm, ROUTER_COLS), lambda i: (i, 0))],
        out_shape=[jax.ShapeDtypeStruct((n, d // 2), jnp.int32), jax.ShapeDtypeStruct((n, ROUTER_COLS), F32)],
        compiler_params=_cparams(("parallel",)),
        name="ffn_norm_router",
    )(h, norm_w.reshape(1, d).astype(F32), jnp.concatenate([rw_hi, rw_lo, rw_hi], axis=0), rb)


def _row_copy(src_hbm, dst_vmem, sem, src_row, slot, dst_row):
    return pltpu.make_async_copy(src_hbm.at[pl.ds(src_row, 1)], dst_vmem.at[slot, pl.ds(dst_row, 1)], sem.at[slot])


def _moe_kernel(blk_e_ref, n_used_ref, tok_ref, f_hbm, w1_ref, w3_ref, w2_ref, gw_ref, o_ref, xs_ref, sem):
    del blk_e_ref
    i = pl.program_id(0)
    n_used = n_used_ref[0]
    rows = xs_ref.shape[1]

    def start(blk, slot):
        def body(r, _):
            _row_copy(f_hbm, xs_ref, sem, tok_ref[blk * rows + r], slot, r).start()
            return 0
        lax.fori_loop(0, rows, body, 0, unroll=8)

    def wait(slot):
        def body(r, _):
            _row_copy(f_hbm, xs_ref, sem, 0, slot, r).wait()
            return 0
        lax.fori_loop(0, rows, body, 0, unroll=8)

    @pl.when(jnp.logical_and(i == 0, n_used > 0))
    def _():
        start(0, 0)

    @pl.when(i + 1 < n_used)
    def _():
        start(i + 1, (i + 1) % 2)

    @pl.when(i < n_used)
    def _():
        slot = i % 2
        wait(slot)
        x = _unpack_bf16_pair(xs_ref[slot]).astype(BF16)
        h1 = jnp.dot(x, w1_ref[...].astype(BF16), preferred_element_type=F32)
        h3 = jnp.dot(x, w3_ref[...].astype(BF16), preferred_element_type=F32)
        hb = (jax.nn.silu(h1) * h3).astype(BF16)
        y = jnp.dot(hb, w2_ref[...].astype(BF16), preferred_element_type=F32) * gw_ref[...]
        half = y.shape[1] // 2
        o_ref[...] = _pack_bf16_pair(y[:, :half], y[:, half:])

    @pl.when(i >= n_used)
    def _():
        o_ref[...] = jnp.zeros_like(o_ref)


def _moe(f, w1, w3, w2, blk_e, n_used, buf_tok, buf_w):
    n = f.shape[0]
    e, d, hdim = w1.shape
    dh = d // 2
    p = buf_tok.shape[0]
    n_blocks = p // MOE_BLOCK
    single = pl.Buffered(1)
    grid_spec = pltpu.PrefetchScalarGridSpec(
        num_scalar_prefetch=3,
        grid=(n_blocks,),
        in_specs=[pl.BlockSpec(memory_space=pl.ANY),
                  pl.BlockSpec((None, d, hdim), lambda i, be, nu, tk: (be[i], 0, 0)),
                  pl.BlockSpec((None, d, hdim), lambda i, be, nu, tk: (be[i], 0, 0)),
                  pl.BlockSpec((None, hdim, d), lambda i, be, nu, tk: (be[i], 0, 0), pipeline_mode=single),
                  pl.BlockSpec((MOE_BLOCK, 1), lambda i, be, nu, tk: (i, 0))],
        out_specs=pl.BlockSpec((MOE_BLOCK, dh), lambda i, be, nu, tk: (i, 0)),
        scratch_shapes=[pltpu.VMEM((2, MOE_BLOCK, dh), jnp.int32), pltpu.SemaphoreType.DMA((2,))],
    )
    return pl.pallas_call(
        _moe_kernel,
        grid_spec=grid_spec,
        out_shape=jax.ShapeDtypeStruct((p, dh), jnp.int32),
        compiler_params=_cparams(("arbitrary",)),
        name="moe_experts",
    )(blk_e, n_used, buf_tok, f, w1, w3, w2, buf_w.reshape(p, 1))


def _combine_kernel(pos_ref, h_ref, ys_hbm, nw_ref, o_ref, yg_ref, sem):
    i = pl.program_id(0)
    n_steps = pl.num_programs(0)
    tm = h_ref.shape[0]
    rows = yg_ref.shape[1]

    def start(step, slot):
        def body(r, _):
            _row_copy(ys_hbm, yg_ref, sem, pos_ref[step * rows + r], slot, r).start()
            return 0
        lax.fori_loop(0, rows, body, 0, unroll=8)

    def wait(slot):
        def body(r, _):
            _row_copy(ys_hbm, yg_ref, sem, 0, slot, r).wait()
            return 0
        lax.fori_loop(0, rows, body, 0, unroll=8)

    @pl.when(i == 0)
    def _():
        start(0, 0)

    @pl.when(i + 1 < n_steps)
    def _():
        start(i + 1, (i + 1) % 2)

    slot = i % 2
    wait(slot)
    h = h_ref[...]
    for k in range(TOP_K_INNER):
        h = h + _unpack_bf16_pair(yg_ref[slot, k * tm:(k + 1) * tm, :])
    y = h * lax.rsqrt(jnp.mean(h * h, axis=-1, keepdims=True) + RMS_EPS)
    o_ref[...] = y * nw_ref[...]


def _combine(h, ys, pos, norm_w):
    n, d = h.shape
    tm = _tile(n, 128)
    grid_spec = pltpu.PrefetchScalarGridSpec(
        num_scalar_prefetch=1,
        grid=(n // tm,),
        in_specs=[pl.BlockSpec((tm, d), lambda i, ps: (i, 0)),
                  pl.BlockSpec(memory_space=pl.ANY),
                  pl.BlockSpec((1, d), lambda i, ps: (0, 0))],
        out_specs=pl.BlockSpec((tm, d), lambda i, ps: (i, 0)),
        scratch_shapes=[pltpu.VMEM((2, TOP_K_INNER * tm, d // 2), jnp.int32), pltpu.SemaphoreType.DMA((2,))],
    )
    return pl.pallas_call(
        _combine_kernel,
        grid_spec=grid_spec,
        out_shape=jax.ShapeDtypeStruct((n, d), F32),
        compiler_params=_cparams(("arbitrary",)),
        name="moe_combine_final_norm",
    )(pos, h, ys, norm_w.reshape(1, d).astype(F32))


def _routing(route, n_tok, tm_combine):
    kk = TOP_K_INNER
    n_exp = N_EXPERT_GROUPS * EXPERTS_PER_GROUP
    blk = MOE_BLOCK
    m = n_tok * kk
    flat_e = route[:, :kk].astype(jnp.int32).reshape(m)
    flat_w = route[:, kk:2 * kk].reshape(m)
    iota_m = jnp.arange(m, dtype=jnp.int32)
    se, order, sw = lax.sort((flat_e, iota_m, flat_w), num_keys=1, is_stable=True)
    counts = jnp.sum((flat_e[:, None] == jnp.arange(n_exp, dtype=jnp.int32)[None, :]).astype(jnp.int32), axis=0)
    padded = ((counts + blk - 1) // blk) * blk
    pad_end = jnp.cumsum(padded)
    pad_start = pad_end - padded
    grp_start = jnp.cumsum(counts) - counts
    n_blocks = -(-m // blk) + n_exp
    p = n_blocks * blk
    blk_e = jnp.minimum(jnp.searchsorted(pad_end, jnp.arange(n_blocks, dtype=jnp.int32) * blk, side='right'),
                        n_exp - 1).astype(jnp.int32)
    n_used = (pad_end[-1] // blk).astype(jnp.int32).reshape(1)
    off = jnp.arange(p, dtype=jnp.int32).reshape(n_blocks, blk) - pad_start[blk_e][:, None]
    valid = off < counts[blk_e][:, None]
    src = jnp.clip(grp_start[blk_e][:, None] + off, 0, m - 1)
    buf_tok = jnp.where(valid, order[src] // kk, 0).reshape(p).astype(jnp.int32)
    buf_w = jnp.where(valid, sw[src], 0.0).reshape(p)
    dest = (pad_start[se] + iota_m - grp_start[se]).astype(jnp.int32)
    _, pos_flat = lax.sort((order, dest), num_keys=1)
    pos = pos_flat.reshape(n_tok // tm_combine, tm_combine, kk).transpose(0, 2, 1).reshape(m)
    return blk_e, n_used, buf_tok, buf_w, pos


def kernel(x, meta_tokens, norm_mix_w, w_in, s5_a_re, s5_a_im, s5_log_dt, s5_b_re, s5_b_im, s5_c_re, s5_c_im, s5_d, s5_glu_w, s5_glu_b, hg_lb_gamma, hg_norm_w, w_branch_a, w_branch_b, w_out, norm_ffn_w, router_group_w, router_group_b, router_expert_w, router_expert_b, expert_w1, expert_w3, expert_w2, norm_final_w):
    n_batch, seq, d_model = x.shape
    depth = w_in.shape[0]
    n_tok = n_batch * seq
    d_s5 = s5_d.shape[-1]
    d_hg = hg_norm_w.shape[-1]
    T = S5_CHUNK
    assert meta_tokens.shape[0] == N_META == T and seq % GLA_TILE == 0
    jb = d_s5 // LANES
    gb = LANES // S5_GROUP
    n_state = s5_a_re.shape[-1]
    sig = jax.nn.sigmoid
    o_u, o_q, o_i, o_f, o_g, o_ga = 0, d_s5, d_s5 + d_hg, d_s5 + 2 * d_hg, d_s5 + 4 * d_hg, d_s5 + 5 * d_hg

    h = x.reshape(n_tok, d_model)
    h_meta = meta_tokens.astype(x.dtype)
    for l in range(depth):
        assert l == 0, "the meta-token shortcut is only valid for a single layer"
        a = _rmsnorm(h, norm_mix_w[l])
        a_m = _rmsnorm(h_meta, norm_mix_w[l])
        w = w_in[l].astype(BF16)
        proj = functools.partial(_matmul, [(a, w)])
        proj_m = functools.partial(_matmul, [(a_m, w)], out_dtype=F32)

        u_blk = proj(n_cols=d_s5, col_off=o_u, lane_blocked=True, name="proj_u")
        q = proj(n_cols=d_hg, col_off=o_q, epilogue=jax.nn.silu, name="proj_q")
        v = proj(n_cols=d_hg, col_off=o_i, name="proj_i")
        z_fb = proj(n_cols=2 * d_hg, col_off=o_f, out_dtype=F32, name="proj_f")
        g_out = proj(n_cols=d_hg, col_off=o_g, epilogue=jax.nn.silu, name="proj_g")
        gates = proj(n_cols=2 * d_model, col_off=o_ga, epilogue=sig, name="proj_gates")
        u_m = proj_m(n_cols=d_s5, col_off=o_u, name="proj_u_meta")
        v_m = proj_m(n_cols=d_hg, col_off=o_i, out_dtype=BF16, name="proj_i_meta")
        zf_m = proj_m(n_cols=d_hg, col_off=o_f, name="proj_f_meta")

        ec, (kf, kb_rev), fc, lam_t = _s5_compact_operators(
            s5_a_re[l].astype(F32), s5_a_im[l].astype(F32), s5_log_dt[l].astype(F32), s5_b_re[l].astype(F32),
            s5_b_im[l].astype(F32), s5_c_re[l].astype(F32), s5_c_im[l].astype(F32), s5_d[l].astype(F32))
        rep_tq = _replicator(T, S5_GROUP, gb)
        rep_dp = _replicator(4, n_state, gb)
        e_mat = _expand(ec, rep_dp, S5_GROUP, n_state, gb, "s5_expand_e")
        m_mat = _expand_toeplitz(kf, kb_rev, rep_tq, T, S5_GROUP, gb, "s5_expand_m")
        f_mat = _expand(fc, rep_tq, n_state, S5_GROUP, gb, "s5_expand_f")
        sw = e_mat.shape[-1]
        u_ch = u_blk.reshape(jb, n_tok // T, T * LANES)
        um = u_m.reshape(T, jb, LANES).transpose(1, 0, 2).reshape(jb, 1, T * LANES)
        um = jnp.pad(um, ((0, 0), (0, 7), (0, 0))).astype(BF16)
        x0 = _matmul([(um, e_mat)], n_cols=sw, out_dtype=F32, name="s5_state_meta")[:, 0:1, :]
        x0 = jnp.concatenate([x0[..., :sw // 2], jnp.zeros_like(x0[..., sw // 2:])], axis=-1)
        s_all = _matmul([(u_ch, e_mat)], n_cols=sw, out_dtype=F32, name="s5_chunk_state")
        x_prev = _s5_scan(s_all, lam_t, x0, n_batch)
        y_blk = _matmul([(u_ch, m_mat), (x_prev, f_mat)], n_cols=T * LANES, name="s5_chunk_out")
        s5_out = _glu(y_blk.reshape(jb, n_tok, LANES), s5_glu_w[l].astype(BF16), s5_glu_b[l])

        lb = jnp.cumsum(jax.nn.softmax(hg_lb_gamma.astype(F32), axis=1), axis=1)[:, l]
        hg_out = _gla(q, v, z_fb, g_out, v_m, zf_m, lb, hg_norm_w[l], n_batch)

        mrg = _merge(s5_out, w_branch_a[l].astype(BF16), hg_out, w_branch_b[l].astype(BF16), gates)
        h = _matmul([(mrg, w_out[l].astype(BF16))], n_cols=d_model, extras=(h,),
                    epilogue=lambda acc, r: acc + r, out_dtype=F32, name="out_proj")

        ng, n_exp = router_group_w.shape[-1], router_expert_w.shape[-1]
        rw = jnp.concatenate([router_group_w[l], router_expert_w[l],
                              jnp.zeros((d_model, ROUTER_COLS - ng - n_exp), F32)], axis=1).astype(F32)
        rb = jnp.concatenate([router_group_b[l], router_expert_b[l],
                              jnp.zeros((ROUTER_COLS - ng - n_exp,), F32)]).reshape(1, ROUTER_COLS).astype(F32)
        assert (ng, n_exp) == (N_EXPERT_GROUPS, N_EXPERT_GROUPS * EXPERTS_PER_GROUP)
        f, route = _router(h, norm_ffn_w[l], rw, rb, ng, n_exp // ng)
        tm_c = _tile(n_tok, 128)
        blk_e, n_used, buf_tok, buf_w, pos = _routing(route, n_tok, tm_c)
        ys = _moe(f, expert_w1[l], expert_w3[l], expert_w2[l], blk_e, n_used, buf_tok, buf_w)
    out = _combine(h, ys, pos, norm_final_w)
    return out.reshape(n_batch, seq, d_model)
```

```python
import functools
import math

import jax
import jax.numpy as jnp
from jax import lax
from jax.experimental import pallas as pl
from jax.experimental.pallas import tpu as pltpu

F32 = jnp.float32
BF16 = jnp.bfloat16

N_META = 16
S5_GROUP = 16
S5_STATE = 64
S5_CHUNK = 16
HG_EXPAND = 128
GLA_CHUNK = 64
GLA_TILE = 256
GLA_SAFE_SPAN = 60.0
N_EXPERT_GROUPS = 8
EXPERTS_PER_GROUP = 8
TOP_K_INNER = 2
MOE_BLOCK = 256
RMS_EPS = 1e-6
LANES = 128
ROUTER_COLS = 128
VMEM_LIMIT = 56 * 1024 * 1024


def _cparams(sem):
    return pltpu.CompilerParams(dimension_semantics=sem, vmem_limit_bytes=VMEM_LIMIT)


def _tile(n, pref):
    t = min(n, pref)
    assert n % t == 0, (n, pref)
    return t


def _rmsnorm_kernel(x_ref, w_ref, o_ref):
    x = x_ref[...]
    y = x * lax.rsqrt(jnp.mean(x * x, axis=-1, keepdims=True) + RMS_EPS)
    o_ref[...] = (y * w_ref[...]).astype(o_ref.dtype)


def _rmsnorm(x, w, out_dtype=BF16):
    n, d = x.shape
    tm = _tile(n, 256)
    return pl.pallas_call(
        _rmsnorm_kernel,
        grid=(n // tm,),
        in_specs=[pl.BlockSpec((tm, d), lambda i: (i, 0)), pl.BlockSpec((1, d), lambda i: (0, 0))],
        out_specs=pl.BlockSpec((tm, d), lambda i: (i, 0)),
        out_shape=jax.ShapeDtypeStruct((n, d), out_dtype),
        compiler_params=_cparams(("parallel",)),
        name="rmsnorm",
    )(x, w.reshape(1, d).astype(F32))


def _mm_kernel(*refs, n_pairs, n_extra, epilogue, lane_blocked):
    a_refs = refs[0:2 * n_pairs:2]
    w_refs = refs[1:2 * n_pairs:2]
    extras = refs[2 * n_pairs:2 * n_pairs + n_extra]
    o_ref = refs[2 * n_pairs + n_extra]
    acc = None
    for a_ref, w_ref in zip(a_refs, w_refs):
        d = jnp.dot(a_ref[...], w_ref[...], preferred_element_type=F32)
        acc = d if acc is None else acc + d
    res = epilogue(acc, *[e[...] for e in extras])
    if lane_blocked:
        scr_ref = refs[2 * n_pairs + n_extra + 1]
        n_fold = o_ref.shape[2] // LANES
        for q in range(o_ref.shape[0]):
            scr_ref[...] = res[:, q * LANES:(q + 1) * LANES]
            for s in range(n_fold):
                o_ref[q, :, s * LANES:(s + 1) * LANES] = scr_ref[pl.ds(s, o_ref.shape[1], stride=n_fold), :].astype(o_ref.dtype)
    else:
        o_ref[...] = res.astype(o_ref.dtype)


def _matmul(pairs, *, n_cols, col_off=0, extras=(), epilogue=None, out_dtype=BF16, lane_blocked=False,
            tm=1024, tn=1024, name="matmul"):
    if epilogue is None:
        epilogue = lambda acc: acc
    a0 = pairs[0][0]
    batched = a0.ndim == 3
    g = a0.shape[0] if batched else 1
    m = a0.shape[-2]
    tm = _tile(m, tm)
    tn = _tile(math.gcd(n_cols, col_off) if col_off else n_cols, tn)
    cb = col_off // tn
    lead = (None,) if batched else ()

    def bidx(b):
        return (b,) if batched else ()

    in_specs, args = [], []
    for a, w in pairs:
        kdim = a.shape[-1]
        assert a.dtype == BF16 and w.dtype == BF16 and w.shape[-2] == kdim and a.shape[-2] == m
        in_specs.append(pl.BlockSpec(lead + (tm, kdim), lambda b, i, j: bidx(b) + (i, 0)))
        in_specs.append(pl.BlockSpec(lead + (kdim, tn), lambda b, i, j: bidx(b) + (0, j + cb)))
        args += [a, w]
    for e in extras:
        if e.shape[-2] == 1:
            in_specs.append(pl.BlockSpec((1, tn), lambda b, i, j: (0, j)))
        else:
            in_specs.append(pl.BlockSpec((tm, tn), lambda b, i, j: (i, j)))
        args.append(e)
    if lane_blocked:
        assert not batched
        fold = S5_CHUNK
        out_shape = jax.ShapeDtypeStruct((n_cols // LANES, m // fold, fold * LANES), out_dtype)
        out_spec = pl.BlockSpec((tn // LANES, tm // fold, fold * LANES), lambda b, i, j: (j, i, 0))
    else:
        out_shape = jax.ShapeDtypeStruct(((g,) if batched else ()) + (m, n_cols), out_dtype)
        out_spec = pl.BlockSpec(lead + (tm, tn), lambda b, i, j: bidx(b) + (i, j))
    return pl.pallas_call(
        functools.partial(_mm_kernel, n_pairs=len(pairs), n_extra=len(extras), epilogue=epilogue,
                          lane_blocked=lane_blocked),
        grid=(g, m // tm, n_cols // tn),
        in_specs=in_specs,
        out_specs=out_spec,
        out_shape=out_shape,
        scratch_shapes=[pltpu.VMEM((tm, LANES), F32)] if lane_blocked else [],
        compiler_params=_cparams(("parallel", "parallel", "parallel")),
        name=name,
    )(*args)


def _s5_compact_operators(a_re, a_im, log_dt, b_re, b_im, c_re, c_im, d_skip):
    T = S5_CHUNK
    hp = lax.Precision.HIGHEST
    G, P = a_re.shape[1:]
    H = b_re.shape[-1]
    GB = LANES // H
    J = G // GB
    dt = jnp.exp(log_dt)[..., None]
    lre, lang = a_re * dt, a_im * dt
    mag = jnp.exp(lre)
    ab_re, ab_im = mag * jnp.cos(lang), mag * jnp.sin(lang)
    den = a_re * a_re + a_im * a_im
    nr, ni = ab_re - 1.0, ab_im
    z_re = ((nr * a_re + ni * a_im) / den)[..., None]
    z_im = ((ni * a_re - nr * a_im) / den)[..., None]
    bb_re = z_re * b_re - z_im * b_im
    bb_im = z_re * b_im + z_im * b_re
    bbt_re, bbt_im = bb_re.transpose(0, 1, 3, 2), bb_im.transpose(0, 1, 3, 2)
    n = jnp.arange(T + 1, dtype=F32)[:, None]
    pmag = jnp.exp(lre[:, :, None, :] * n)
    pw_re = pmag * jnp.cos(lang[:, :, None, :] * n)
    pw_im = pmag * jnp.sin(lang[:, :, None, :] * n)
    cl_re = c_re[:, :, None] * pw_re[:, :, :, None, :] - c_im[:, :, None] * pw_im[:, :, :, None, :]
    cl_im = c_re[:, :, None] * pw_im[:, :, :, None, :] + c_im[:, :, None] * pw_re[:, :, :, None, :]
    lhs = jnp.concatenate([bbt_re, -bbt_im], axis=-1)
    rhs = jnp.concatenate([cl_re, cl_im], axis=-1).reshape(2, G, (T + 1) * H, 2 * P)
    kk = jnp.einsum('dghk,dgmk->dghm', lhs, rhs, precision=hp).reshape(2, G, H, T + 1, H)
    eye_h = jnp.eye(H, dtype=F32)
    kk_f = kk[0].at[:, :, 0, :].add(eye_h[None] * d_skip.reshape(G, H)[:, :, None])
    kf = kk_f[:, :, :T].reshape(J, LANES, T * H)
    kb_rev = jnp.flip(kk[1][:, :, :T], axis=2).reshape(J, LANES, T * H)

    ar = jnp.arange(T)
    e_parts = []
    for d, idx in ((0, T - 1 - ar), (1, ar)):
        pr, pi = pw_re[d][:, idx, None, :], pw_im[d][:, idx, None, :]
        br, bi = bbt_re[d][:, None], bbt_im[d][:, None]
        e_parts += [pr * br - pi * bi, pr * bi + pi * br]
    ec = jnp.stack(e_parts, axis=3).reshape(J, GB, T, H, 4 * P).transpose(0, 2, 1, 3, 4).reshape(J, T * LANES, 4 * P)

    ct_re, ct_im = c_re.transpose(0, 1, 3, 2), c_im.transpose(0, 1, 3, 2)
    pt_re, pt_im = pw_re.transpose(0, 1, 3, 2), pw_im.transpose(0, 1, 3, 2)
    f_parts = []
    for d, idx in ((0, ar + 1), (1, T - ar)):
        qr, qi = pt_re[d][:, :, idx, None], pt_im[d][:, :, idx, None]
        cr, ci = ct_re[d][:, :, None, :], ct_im[d][:, :, None, :]
        f_parts += [cr * qr - ci * qi, -(cr * qi + ci * qr)]
    fc = jnp.stack(f_parts, axis=0).reshape(4, J, GB * P, T * H).transpose(1, 0, 2, 3).reshape(J, 4 * GB * P, T * H)

    lam_t = jnp.stack([pw_re[0][:, T], pw_im[0][:, T], pw_re[1][:, T], pw_im[1][:, T]], 0)
    lam_t = lam_t.reshape(4, J, GB * P).transpose(1, 0, 2).reshape(J, 1, 4 * GB * P)
    return ec.astype(BF16), (kf, kb_rev), fc.astype(BF16), lam_t


def _replicator(n_outer, n_inner, reps):
    rows = jnp.arange(n_outer * n_inner)
    cols = jnp.arange(n_outer * reps * n_inner)
    same_o = (rows[:, None] // n_inner) == (cols[None, :] // (reps * n_inner))
    same_i = (rows[:, None] % n_inner) == (cols[None, :] % n_inner)
    return (same_o & same_i).astype(BF16)


def _expand_kernel(c_ref, r_ref, o_ref, *, row_group, col_group, n_groups):
    r, tn = o_ref.shape
    rep = jnp.dot(c_ref[...], r_ref[...], preferred_element_type=F32)
    rows = lax.broadcasted_iota(jnp.int32, (r, 1), 0)
    cols = lax.broadcasted_iota(jnp.int32, (1, tn), 1) + pl.program_id(1) * tn
    row_g = (rows >> int(math.log2(row_group))) & (n_groups - 1)
    col_g = (cols >> int(math.log2(col_group))) & (n_groups - 1)
    o_ref[...] = jnp.where(row_g == col_g, rep, 0.0).astype(o_ref.dtype)


def _expand(compact, rmat, row_group, col_group, n_groups, name):
    j, r, c = compact.shape
    n_cols = rmat.shape[1]
    tn = _tile(n_cols, 1024)
    return pl.pallas_call(
        functools.partial(_expand_kernel, row_group=row_group, col_group=col_group, n_groups=n_groups),
        grid=(j, n_cols // tn),
        in_specs=[pl.BlockSpec((None, r, c), lambda i, k: (i, 0, 0)), pl.BlockSpec((c, tn), lambda i, k: (0, k))],
        out_specs=pl.BlockSpec((None, r, tn), lambda i, k: (i, 0, k)),
        out_shape=jax.ShapeDtypeStruct((j, r, n_cols), BF16),
        compiler_params=_cparams(("parallel", "parallel")),
        name=name,
    )(compact, rmat)


def _expand_toeplitz_kernel(kf_ref, kb_ref, r_ref, o_ref, *, n_tok, lag_width, group, n_groups):
    rb, w = kf_ref.shape
    tn = o_ref.shape[-1]
    kf, kb = kf_ref[...], kb_ref[...]
    lane = lax.broadcasted_iota(jnp.int32, (rb, w), 1)
    rows = lax.broadcasted_iota(jnp.int32, (rb, 1), 0)
    cols = lax.broadcasted_iota(jnp.int32, (1, tn), 1) + pl.program_id(1) * tn
    shift = int(math.log2(group))
    keep = ((rows >> shift) & (n_groups - 1)) == ((cols >> shift) & (n_groups - 1))
    for s in range(n_tok):
        lo, hi = s * lag_width, (s + 1) * lag_width
        fwd = jnp.where(lane >= lo, kf if lo == 0 else pltpu.roll(kf, lo, 1), 0.0)
        bwd = jnp.where(lane < hi, kb if hi == w else pltpu.roll(kb, hi, 1), 0.0)
        rep = jnp.dot((fwd + bwd).astype(BF16), r_ref[...], preferred_element_type=F32)
        o_ref[s * rb:(s + 1) * rb, :] = jnp.where(keep, rep, 0.0).astype(o_ref.dtype)


def _expand_toeplitz(kf, kb_rev, rmat, n_tok, group, n_groups, name):
    j, rb, w = kf.shape
    n_cols = rmat.shape[1]
    tn = _tile(n_cols, 1024)
    lag = pl.BlockSpec((None, rb, w), lambda i, k: (i, 0, 0))
    return pl.pallas_call(
        functools.partial(_expand_toeplitz_kernel, n_tok=n_tok, lag_width=w // n_tok, group=group, n_groups=n_groups),
        grid=(j, n_cols // tn),
        in_specs=[lag, lag, pl.BlockSpec((w, tn), lambda i, k: (0, k))],
        out_specs=pl.BlockSpec((None, n_tok * rb, tn), lambda i, k: (i, 0, k)),
        out_shape=jax.ShapeDtypeStruct((j, n_tok * rb, n_cols), BF16),
        compiler_params=_cparams(("parallel", "parallel")),
        name=name,
    )(kf, kb_rev, rmat)


def _s5_scan_kernel(s_ref, lam_ref, x0_ref, o_ref, xp_ref, *, n_batch, n_chunks):
    hw = s_ref.shape[-1] // 4
    lam = lam_ref[...]
    lfr, lfi, lbr, lbi = (lam[:, q * hw:(q + 1) * hw] for q in range(4))
    x0 = x0_ref[...]
    zeros = jnp.zeros((1, hw), F32)
    init = tuple((x0[:, 0:hw], x0[:, hw:2 * hw], zeros, zeros) for _ in range(n_batch))

    def step(c, carry):
        out = []
        for b, (xfr, xfi, xbr, xbi) in enumerate(carry):
            row_f = pl.ds(b * n_chunks + c, 1)
            row_b = pl.ds(b * n_chunks + n_chunks - 1 - c, 1)
            xp_ref[row_f, 0:hw] = xfr
            xp_ref[row_f, hw:2 * hw] = xfi
            xp_ref[row_b, 2 * hw:3 * hw] = xbr
            xp_ref[row_b, 3 * hw:4 * hw] = xbi
            sf = s_ref[row_f, 0:2 * hw]
            sb = s_ref[row_b, 2 * hw:4 * hw]
            out.append((lfr * xfr - lfi * xfi + sf[:, 0:hw],
                        lfr * xfi + lfi * xfr + sf[:, hw:2 * hw],
                        lbr * xbr - lbi * xbi + sb[:, 0:hw],
                        lbr * xbi + lbi * xbr + sb[:, hw:2 * hw]))
        return tuple(out)

    lax.fori_loop(0, n_chunks, step, init)
    o_ref[...] = xp_ref[...].astype(o_ref.dtype)


def _s5_scan(s_all, lam_t, x0, n_batch):
    j, r, w = s_all.shape
    n_chunks = r // n_batch
    return pl.pallas_call(
        functools.partial(_s5_scan_kernel, n_batch=n_batch, n_chunks=n_chunks),
        grid=(j,),
        in_specs=[pl.BlockSpec((None, r, w), lambda i: (i, 0, 0)),
                  pl.BlockSpec((None, 1, w), lambda i: (i, 0, 0)),
                  pl.BlockSpec((None, 1, w), lambda i: (i, 0, 0))],
        out_specs=pl.BlockSpec((None, r, w), lambda i: (i, 0, 0)),
        out_shape=jax.ShapeDtypeStruct((j, r, w), BF16),
        scratch_shapes=[pltpu.VMEM((r, w), F32)],
        compiler_params=_cparams(("parallel",)),
        name="s5_scan",
    )(s_all, lam_t, x0)


def _glu_kernel(y_ref, w_ref, b_ref, o_ref, z_ref, ys_ref):
    jn = pl.program_id(1)
    tn = o_ref.shape[-1]

    @pl.when(jn == 0)
    def _():
        n_fold = y_ref.shape[2] // LANES
        for q in range(y_ref.shape[0]):
            for s in range(n_fold):
                ys_ref[pl.ds(s, y_ref.shape[1], stride=n_fold), :] = y_ref[q, :, s * LANES:(s + 1) * LANES].astype(F32)
            z_ref[:, q * LANES:(q + 1) * LANES] = jax.nn.gelu(ys_ref[...]).astype(z_ref.dtype)

    gate = jnp.dot(z_ref[...], w_ref[...], preferred_element_type=F32) + b_ref[...]
    zc = z_ref[:, pl.ds(pl.multiple_of(jn * tn, tn), tn)].astype(F32)
    o_ref[...] = (zc * jax.nn.sigmoid(gate)).astype(o_ref.dtype)


def _glu(y_ch, glu_w, glu_b):
    jb, n_ch, w_ch = y_ch.shape
    fold = w_ch // LANES
    n = n_ch * fold
    d = jb * LANES
    tm, tn = _tile(n, 1024), _tile(d, 1024)
    return pl.pallas_call(
        _glu_kernel,
        grid=(n // tm, d // tn),
        in_specs=[pl.BlockSpec((jb, tm // fold, w_ch), lambda i, j: (0, i, 0)),
                  pl.BlockSpec((d, tn), lambda i, j: (0, j)),
                  pl.BlockSpec((1, tn), lambda i, j: (0, j))],
        out_specs=pl.BlockSpec((tm, tn), lambda i, j: (i, j)),
        out_shape=jax.ShapeDtypeStruct((n, d), BF16),
        scratch_shapes=[pltpu.VMEM((tm, d), BF16), pltpu.VMEM((tm, LANES), F32)],
        compiler_params=_cparams(("parallel", "arbitrary")),
        name="s5_glu",
    )(y_ch, glu_w, glu_b.reshape(1, d).astype(F32))


def _nt(a, b):
    return lax.dot_general(a, b, (((1,), (1,)), ((), ())), preferred_element_type=F32)


def _split3(x):
    hi = x.astype(BF16)
    r1 = x - hi.astype(F32)
    mid = r1.astype(BF16)
    lo = (r1 - mid.astype(F32)).astype(BF16)
    return jnp.concatenate([hi, mid, lo], axis=0)


def _gates(z, lbd):
    f = lbd + (1.0 - lbd) * jax.nn.sigmoid(z)
    return 1.0 - f, jnp.log(f)


def _gla_kernel(q_ref, v_ref, zf_ref, zb_ref, go_ref, vm_ref, zm_ref, lb_ref, nw_ref, o_ref,
                acc_ref, vt_ref, k_ref, g3_ref, b_ref, qt_ref, kt_ref, qe_ref, kh_ref, u_ref, dl_ref, st_ref, s_ref):
    L, K = q_ref.shape
    C = GLA_CHUNK
    TR = GLA_TILE
    cpt = TR // C
    n_chunks = L // C
    n_tiles = L // TR
    lb = lb_ref[...]
    shift_c, shift_k = int(math.log2(C)), int(math.log2(K))
    row = lax.broadcasted_iota(jnp.int32, (TR, TR), 0)
    col = lax.broadcasted_iota(jnp.int32, (TR, TR), 1)
    same_chunk = (row >> shift_c) == (col >> shift_c)
    own_block = ((lax.broadcasted_iota(jnp.int32, (TR, cpt * K), 0) >> shift_c)
                 == (lax.broadcasted_iota(jnp.int32, (TR, cpt * K), 1) >> shift_k))

    def crow(c):
        return pl.ds(pl.multiple_of(c * C, C), C)

    def trow(t):
        return pl.ds(pl.multiple_of(t * TR, TR), TR)

    def block_diag(x):
        return jnp.where(own_block, jnp.concatenate([x] * cpt, axis=1), jnp.zeros((), x.dtype))

    def transpose_v(t, _):
        vt_ref[t] = v_ref[trow(t), :].astype(F32).T.astype(BF16)
        return 0

    lax.fori_loop(0, n_tiles, transpose_v, 0, unroll=2)

    m = zm_ref.shape[0]
    km, gm = _gates(zm_ref[...], lb[0:1, :])
    mrow = lax.broadcasted_iota(jnp.int32, (m, m), 0)
    mcol = lax.broadcasted_iota(jnp.int32, (m, m), 1)
    bm = jnp.dot(jnp.concatenate([(mrow >= mcol).astype(BF16)] * 3, axis=1), _split3(gm),
                 preferred_element_type=F32)
    khm = (km * jnp.exp(bm[m - 1:m, :] - bm)).astype(BF16)
    st0 = jnp.dot(vm_ref[...].astype(F32).T.astype(BF16), khm, preferred_element_type=F32)

    for d, z_ref in enumerate((zf_ref, zb_ref)):
        lbd = lb[d:d + 1, :]
        causal = (row >= col) if d == 0 else (col >= row)
        keep = same_chunk & causal
        cum = jnp.concatenate([causal[:C, :C].astype(BF16)] * 3, axis=1)
        ref_row = C // 2 - 1 if d == 0 else C // 2
        edge_row = C - 1 if d == 0 else 0

        def gate_stage(t, _):
            k, g = _gates(z_ref[trow(t), :], lbd)
            k_ref[trow(t), :] = k
            hi = g.astype(BF16)
            r1 = g - hi.astype(F32)
            mid = r1.astype(BF16)
            lo = (r1 - mid.astype(F32)).astype(BF16)
            for part, val in enumerate((hi, mid, lo)):
                for j in range(cpt):
                    g3_ref[t, part * C:(part + 1) * C, j * K:(j + 1) * K] = val[j * C:(j + 1) * C, :]
            return 0

        lax.fori_loop(0, n_tiles, gate_stage, 0, unroll=2)

        def cumsum_stage(t, _):
            bw = jnp.dot(cum, g3_ref[t], preferred_element_type=F32)
            for j in range(cpt):
                b_ref[crow(t * cpt + j), :] = bw[:, j * K:(j + 1) * K]
            return 0

        lax.fori_loop(0, n_tiles, cumsum_stage, 0, unroll=4)

        def scale_stage(c, worst):
            rows = crow(c)
            b = b_ref[rows, :]
            r = b[ref_row:ref_row + 1, :]
            e = b[edge_row:edge_row + 1, :]
            q = q_ref[rows, :].astype(F32)
            k = k_ref[rows, :]
            qt_ref[rows, :] = (q * jnp.exp(b - r)).astype(BF16)
            kt_ref[rows, :] = (k * jnp.exp(r - b)).astype(BF16)
            qe_ref[rows, :] = (q * jnp.exp(b)).astype(BF16)
            kh_ref[rows, :] = (k * jnp.exp(e - b)).astype(BF16)
            dl_ref[pl.ds(c, 1), :] = jnp.exp(e)
            span = jnp.maximum(jnp.abs(b[0:1, :] - r), jnp.abs(b[C - 1:C, :] - r))
            return jnp.maximum(worst, span)

        worst = lax.fori_loop(0, n_chunks, scale_stage, jnp.zeros((1, K), F32), unroll=4)

        def score_stage(t, _):
            rows = trow(t)
            s_ref[rows, :] = jnp.where(keep, _nt(qt_ref[rows, :], kt_ref[rows, :]), 0.0).astype(BF16)
            return 0

        lax.fori_loop(0, n_tiles, score_stage, 0, unroll=4)

        @pl.when(jnp.max(worst) > GLA_SAFE_SPAN)
        def _():
            srow = lax.broadcasted_iota(jnp.int32, (C, 1), 0)
            lane_t = lax.broadcasted_iota(jnp.int32, (C, K), 1)

            def exact_tile(ti, _):
                for j in range(cpt):
                    rows = pl.ds(pl.multiple_of(ti * TR + j * C, C), C)
                    b = b_ref[rows, :]
                    qc = q_ref[rows, :].astype(F32)
                    kc = k_ref[rows, :]

                    def token(t, st):
                        pick = srow == t
                        bt = jnp.sum(jnp.where(pick, b, 0.0), axis=0, keepdims=True)
                        qrow = jnp.sum(jnp.where(pick, qc, 0.0), axis=0, keepdims=True)
                        col = jnp.sum(qrow * kc * jnp.exp(jnp.minimum(bt - b, 0.0)), axis=-1, keepdims=True)
                        seen = (srow <= t) if d == 0 else (srow >= t)
                        return jnp.where(lane_t == t, jnp.where(seen, col, 0.0), st)

                    blk = lax.fori_loop(0, C, token, jnp.zeros((C, K), F32)).T[:C, :]
                    zero = jnp.zeros((C, C), F32)
                    s_ref[rows, :] = jnp.concatenate([zero] * j + [blk] + [zero] * (cpt - 1 - j), axis=1).astype(BF16)
                return 0

            lax.fori_loop(0, n_tiles, exact_tile, 0)

        def intra_stage(t, _):
            rows = trow(t)
            acc_ref[d, rows, :] = jnp.dot(s_ref[rows, :], v_ref[rows, :], preferred_element_type=F32)
            return 0

        lax.fori_loop(0, n_tiles, intra_stage, 0, unroll=4)

        def incr_stage(t, _):
            u_all = jnp.dot(vt_ref[t], block_diag(kh_ref[trow(t), :]), preferred_element_type=F32)
            for j in range(cpt):
                u_ref[t * cpt + j] = u_all[:, j * K:(j + 1) * K]
            return 0

        lax.fori_loop(0, n_tiles, incr_stage, 0, unroll=4)

        def state_stage(i, st):
            c = i if d == 0 else n_chunks - 1 - i
            st_ref[c] = st.astype(BF16)
            return st * dl_ref[pl.ds(c, 1), :] + u_ref[c]

        lax.fori_loop(0, n_chunks, state_stage, st0 if d == 0 else jnp.zeros((K, K), F32), unroll=4)

        def inter_stage(t, _):
            rows = trow(t)
            st_cat = jnp.concatenate([st_ref[t * cpt + j] for j in range(cpt)], axis=1)
            acc_ref[d, rows, :] += _nt(block_diag(qe_ref[rows, :]), st_cat)
            return 0

        lax.fori_loop(0, n_tiles, inter_stage, 0, unroll=4)

    o = acc_ref[0] + acc_ref[1]
    o = o * lax.rsqrt(jnp.mean(o * o, axis=-1, keepdims=True) + RMS_EPS)
    o_ref[...] = (o * nw_ref[...] * go_ref[...].astype(F32)).astype(o_ref.dtype)


def _gla(q, v, z_fb, go, v_meta, zf_meta, lb, norm_w, n_batch):
    n, d = q.shape
    L = n // n_batch
    heads = d // HG_EXPAND
    m = v_meta.shape[0]
    C = GLA_CHUNK
    n_chunks = L // C
    hk = HG_EXPAND
    tok = pl.BlockSpec((L, hk), lambda b, h: (b, h))
    tok_bwd = pl.BlockSpec((L, hk), lambda b, h: (b, h + heads))
    return pl.pallas_call(
        _gla_kernel,
        grid=(n_batch, heads),
        in_specs=[tok, tok, tok, tok_bwd, tok,
                  pl.BlockSpec((m, hk), lambda b, h: (0, h)),
                  pl.BlockSpec((m, hk), lambda b, h: (0, h)),
                  pl.BlockSpec((2, hk), lambda b, h: (0, h)),
                  pl.BlockSpec((1, hk), lambda b, h: (0, h))],
        out_specs=tok,
        out_shape=jax.ShapeDtypeStruct((n, d), BF16),
        scratch_shapes=[pltpu.VMEM((2, L, hk), F32),
                        pltpu.VMEM((L // GLA_TILE, hk, GLA_TILE), BF16),
                        pltpu.VMEM((L, hk), F32),
                        pltpu.VMEM((L // GLA_TILE, 3 * C, (GLA_TILE // C) * hk), BF16),
                        pltpu.VMEM((L, hk), F32),
                        pltpu.VMEM((L, hk), BF16),
                        pltpu.VMEM((L, hk), BF16),
                        pltpu.VMEM((L, hk), BF16),
                        pltpu.VMEM((L, hk), BF16),
                        pltpu.VMEM((n_chunks, hk, hk), F32),
                        pltpu.VMEM((n_chunks, hk), F32),
                        pltpu.VMEM((n_chunks, hk, hk), BF16),
                        pltpu.VMEM((L, GLA_TILE), BF16)],
        compiler_params=_cparams(("parallel", "parallel")),
        name="hgrn2_gla",
    )(q, v, z_fb, z_fb, go, v_meta, zf_meta, lb, norm_w.reshape(1, d).astype(F32))


def _merge_kernel(a1_ref, w1_ref, a2_ref, w2_ref, g1_ref, g2_ref, o_ref):
    y1 = jnp.dot(a1_ref[...], w1_ref[...], preferred_element_type=F32)
    y2 = jnp.dot(a2_ref[...], w2_ref[...], preferred_element_type=F32)
    o_ref[...] = (g1_ref[...].astype(F32) * y1 + g2_ref[...].astype(F32) * y2).astype(o_ref.dtype)


def _merge(a1, w1, a2, w2, gates):
    n, k1 = a1.shape
    k2 = a2.shape[1]
    d = w1.shape[1]
    tm, tn = _tile(n, 1024), _tile(d, 1024)
    nj = d // tn
    o_spec = pl.BlockSpec((tm, tn), lambda i, j: (i, j))
    return pl.pallas_call(
        _merge_kernel,
        grid=(n // tm, nj),
        in_specs=[pl.BlockSpec((tm, k1), lambda i, j: (i, 0)), pl.BlockSpec((k1, tn), lambda i, j: (0, j)),
                  pl.BlockSpec((tm, k2), lambda i, j: (i, 0)), pl.BlockSpec((k2, tn), lambda i, j: (0, j)),
                  o_spec, pl.BlockSpec((tm, tn), lambda i, j: (i, j + nj))],
        out_specs=o_spec,
        out_shape=jax.ShapeDtypeStruct((n, d), BF16),
        compiler_params=_cparams(("parallel", "parallel")),
        name="branch_merge",
    )(a1, w1, a2, w2, gates, gates)


def _pack_bf16_pair(lo, hi):
    lo_bits = pltpu.bitcast(lo.astype(BF16).astype(F32), jnp.int32)
    hi_bits = pltpu.bitcast(hi.astype(BF16).astype(F32), jnp.int32)
    return lax.shift_right_logical(lo_bits, 16) | hi_bits


def _unpack_bf16_pair(w):
    lo = pltpu.bitcast(w << 16, F32)
    hi = pltpu.bitcast(w & jnp.int32(-65536), F32)
    return jnp.concatenate([lo, hi], axis=-1)


def _router_kernel(h_ref, nw_ref, rw_ref, rb_ref, f_ref, rt_ref, *, n_groups, per_group):
    x = h_ref[...]
    f = x * lax.rsqrt(jnp.mean(x * x, axis=-1, keepdims=True) + RMS_EPS) * nw_ref[...]
    half = f.shape[1] // 2
    f_ref[...] = _pack_bf16_pair(f[:, :half], f[:, half:])
    f_hi = f.astype(BF16)
    f_lo = (f - f_hi.astype(F32)).astype(BF16)
    logits = jnp.dot(jnp.concatenate([f_hi, f_hi, f_lo], axis=1), rw_ref[...], preferred_element_type=F32) + rb_ref[...]
    lane = lax.broadcasted_iota(jnp.int32, logits.shape, 1)
    lane_f = lane.astype(F32)
    big = float(logits.shape[1])
    neg = -jnp.inf

    def first_argmax(v):
        vmax = jnp.max(v, axis=-1, keepdims=True)
        return vmax, jnp.min(jnp.where(v == vmax, lane_f, big), axis=-1, keepdims=True)

    lg = jnp.where(lane < n_groups, logits, neg)
    g_max, g_sel = first_argmax(lg)
    g_prob = 1.0 / jnp.sum(jnp.exp(lg - g_max), axis=-1, keepdims=True)
    lo = n_groups + g_sel * per_group
    le = jnp.where((lane_f >= lo) & (lane_f < lo + per_group), logits, neg)
    v1, i1 = first_argmax(le)
    v2, i2 = first_argmax(jnp.where(lane_f == i1, neg, le))
    t = jnp.exp(v2 - v1)
    w1 = g_prob / (1.0 + t)
    w2 = g_prob * t / (1.0 + t)
    rt = jnp.where(lane == 0, i1 - n_groups, jnp.where(lane == 1, i2 - n_groups,
                   jnp.where(lane == 2, w1, jnp.where(lane == 3, w2, 0.0))))
    rt_ref[...] = rt


def _router(h, norm_w, rw, rb, n_groups, per_group):
    n, d = h.shape
    tm = _tile(n, 256)
    rw_hi = rw.astype(BF16)
    rw_lo = (rw - rw_hi.astype(F32)).astype(BF16)
    return pl.pallas_call(
        functools.partial(_router_kernel, n_groups=n_groups, per_group=per_group),
        grid=(n // tm,),
        in_specs=[pl.BlockSpec((tm, d), lambda i: (i, 0)), pl.BlockSpec((1, d), lambda i: (0, 0)),
                  pl.BlockSpec((3 * d, ROUTER_COLS), lambda i: (0, 0)),
                  pl.BlockSpec((1, ROUTER_COLS), lambda i: (0, 0))],
        out_specs=[pl.BlockSpec((tm, d // 2), lambda i: (i, 0)), pl.BlockSpec((tm, ROUTER_COLS), lambda i: (i, 0))],
        out_shape=[jax.ShapeDtypeStruct((n, d // 2), jnp.int32), jax.ShapeDtypeStruct((n, ROUTER_COLS), F32)],
        compiler_params=_cparams(("parallel",)),
        name="ffn_norm_router",
    )(h, norm_w.reshape(1, d).astype(F32), jnp.concatenate([rw_hi, rw_lo, rw_hi], axis=0), rb)


def _row_copy(src_hbm, dst_vmem, sem, src_row, slot, dst_row):
    return pltpu.make_async_copy(src_hbm.at[pl.ds(src_row, 1)], dst_vmem.at[slot, pl.ds(dst_row, 1)], sem.at[slot])


def _moe_kernel(blk_e_ref, n_used_ref, tok_ref, f_hbm, w1_ref, w3_ref, w2_ref, gw_ref, o_ref, xs_ref, sem):
    del blk_e_ref
    i = pl.program_id(0)
    n_used = n_used_ref[0]
    rows = xs_ref.shape[1]

    def start(blk, slot):
        def body(r, _):
            _row_copy(f_hbm, xs_ref, sem, tok_ref[blk * rows + r], slot, r).start()
            return 0
        lax.fori_loop(0, rows, body, 0, unroll=8)

    def wait(slot):
        def body(r, _):
            _row_copy(f_hbm, xs_ref, sem, 0, slot, r).wait()
            return 0
        lax.fori_loop(0, rows, body, 0, unroll=8)

    @pl.when(jnp.logical_and(i == 0, n_used > 0))
    def _():
        start(0, 0)

    @pl.when(i + 1 < n_used)
    def _():
        start(i + 1, (i + 1) % 2)

    @pl.when(i < n_used)
    def _():
        slot = i % 2
        wait(slot)
        x = _unpack_bf16_pair(xs_ref[slot]).astype(BF16)
        h1 = jnp.dot(x, w1_ref[...].astype(BF16), preferred_element_type=F32)
        h3 = jnp.dot(x, w3_ref[...].astype(BF16), preferred_element_type=F32)
        hb = (jax.nn.silu(h1) * h3).astype(BF16)
        y = jnp.dot(hb, w2_ref[...].astype(BF16), preferred_element_type=F32) * gw_ref[...]
        half = y.shape[1] // 2
        o_ref[...] = _pack_bf16_pair(y[:, :half], y[:, half:])

    @pl.when(i >= n_used)
    def _():
        o_ref[...] = jnp.zeros_like(o_ref)


def _moe(f, w1, w3, w2, blk_e, n_used, buf_tok, buf_w):
    n = f.shape[0]
    e, d, hdim = w1.shape
    dh = d // 2
    p = buf_tok.shape[0]
    n_blocks = p // MOE_BLOCK
    single = pl.Buffered(1)
    grid_spec = pltpu.PrefetchScalarGridSpec(
        num_scalar_prefetch=3,
        grid=(n_blocks,),
        in_specs=[pl.BlockSpec(memory_space=pl.ANY),
                  pl.BlockSpec((None, d, hdim), lambda i, be, nu, tk: (be[i], 0, 0)),
                  pl.BlockSpec((None, d, hdim), lambda i, be, nu, tk: (be[i], 0, 0)),
                  pl.BlockSpec((None, hdim, d), lambda i, be, nu, tk: (be[i], 0, 0), pipeline_mode=single),
                  pl.BlockSpec((MOE_BLOCK, 1), lambda i, be, nu, tk: (i, 0))],
        out_specs=pl.BlockSpec((MOE_BLOCK, dh), lambda i, be, nu, tk: (i, 0)),
        scratch_shapes=[pltpu.VMEM((2, MOE_BLOCK, dh), jnp.int32), pltpu.SemaphoreType.DMA((2,))],
    )
    return pl.pallas_call(
        _moe_kernel,
        grid_spec=grid_spec,
        out_shape=jax.ShapeDtypeStruct((p, dh), jnp.int32),
        compiler_params=_cparams(("arbitrary",)),
        name="moe_experts",
    )(blk_e, n_used, buf_tok, f, w1, w3, w2, buf_w.reshape(p, 1))


def _combine_kernel(pos_ref, h_ref, ys_hbm, nw_ref, o_ref, yg_ref, sem):
    i = pl.program_id(0)
    n_steps = pl.num_programs(0)
    tm = h_ref.shape[0]
    rows = yg_ref.shape[1]

    def start(step, slot):
        def body(r, _):
            _row_copy(ys_hbm, yg_ref, sem, pos_ref[step * rows + r], slot, r).start()
            return 0
        lax.fori_loop(0, rows, body, 0, unroll=8)

    def wait(slot):
        def body(r, _):
            _row_copy(ys_hbm, yg_ref, sem, 0, slot, r).wait()
            return 0
        lax.fori_loop(0, rows, body, 0, unroll=8)

    @pl.when(i == 0)
    def _():
        start(0, 0)

    @pl.when(i + 1 < n_steps)
    def _():
        start(i + 1, (i + 1) % 2)

    slot = i % 2
    wait(slot)
    h = h_ref[...]
    for k in range(TOP_K_INNER):
        h = h + _unpack_bf16_pair(yg_ref[slot, k * tm:(k + 1) * tm, :])
    y = h * lax.rsqrt(jnp.mean(h * h, axis=-1, keepdims=True) + RMS_EPS)
    o_ref[...] = y * nw_ref[...]


def _combine(h, ys, pos, norm_w):
    n, d = h.shape
    tm = _tile(n, 128)
    grid_spec = pltpu.PrefetchScalarGridSpec(
        num_scalar_prefetch=1,
        grid=(n // tm,),
        in_specs=[pl.BlockSpec((tm, d), lambda i, ps: (i, 0)),
                  pl.BlockSpec(memory_space=pl.ANY),
                  pl.BlockSpec((1, d), lambda i, ps: (0, 0))],
        out_specs=pl.BlockSpec((tm, d), lambda i, ps: (i, 0)),
        scratch_shapes=[pltpu.VMEM((2, TOP_K_INNER * tm, d // 2), jnp.int32), pltpu.SemaphoreType.DMA((2,))],
    )
    return pl.pallas_call(
        _combine_kernel,
        grid_spec=grid_spec,
        out_shape=jax.ShapeDtypeStruct((n, d), F32),
        compiler_params=_cparams(("arbitrary",)),
        name="moe_combine_final_norm",
    )(pos, h, ys, norm_w.reshape(1, d).astype(F32))


def _routing(route, n_tok, tm_combine):
    kk = TOP_K_INNER
    n_exp = N_EXPERT_GROUPS * EXPERTS_PER_GROUP
    blk = MOE_BLOCK
    m = n_tok * kk
    flat_e = route[:, :kk].astype(jnp.int32).reshape(m)
    flat_w = route[:, kk:2 * kk].reshape(m)
    iota_m = jnp.arange(m, dtype=jnp.int32)
    se, order, sw = lax.sort((flat_e, iota_m, flat_w), num_keys=1, is_stable=True)
    counts = jnp.sum((flat_e[:, None] == jnp.arange(n_exp, dtype=jnp.int32)[None, :]).astype(jnp.int32), axis=0)
    padded = ((counts + blk - 1) // blk) * blk
    pad_end = jnp.cumsum(padded)
    pad_start = pad_end - padded
    grp_start = jnp.cumsum(counts) - counts
    n_blocks = -(-m // blk) + n_exp
    p = n_blocks * blk
    blk_e = jnp.minimum(jnp.searchsorted(pad_end, jnp.arange(n_blocks, dtype=jnp.int32) * blk, side='right'),
                        n_exp - 1).astype(jnp.int32)
    n_used = (pad_end[-1] // blk).astype(jnp.int32).reshape(1)
    off = jnp.arange(p, dtype=jnp.int32).reshape(n_blocks, blk) - pad_start[blk_e][:, None]
    valid = off < counts[blk_e][:, None]
    src = jnp.clip(grp_start[blk_e][:, None] + off, 0, m - 1)
    buf_tok = jnp.where(valid, order[src] // kk, 0).reshape(p).astype(jnp.int32)
    buf_w = jnp.where(valid, sw[src], 0.0).reshape(p)
    dest = ((pad_start - grp_start)[se] + iota_m).astype(jnp.int32)
    _, pos_flat = lax.sort((order, dest), num_keys=1)
    pos = pos_flat.reshape(n_tok // tm_combine, tm_combine, kk).transpose(0, 2, 1).reshape(m)
    return blk_e, n_used, buf_tok, buf_w, pos


def kernel(x, meta_tokens, norm_mix_w, w_in, s5_a_re, s5_a_im, s5_log_dt, s5_b_re, s5_b_im, s5_c_re, s5_c_im, s5_d, s5_glu_w, s5_glu_b, hg_lb_gamma, hg_norm_w, w_branch_a, w_branch_b, w_out, norm_ffn_w, router_group_w, router_group_b, router_expert_w, router_expert_b, expert_w1, expert_w3, expert_w2, norm_final_w):
    n_batch, seq, d_model = x.shape
    depth = w_in.shape[0]
    n_tok = n_batch * seq
    d_s5 = s5_d.shape[-1]
    d_hg = hg_norm_w.shape[-1]
    T = S5_CHUNK
    assert meta_tokens.shape[0] == N_META == T and seq % GLA_TILE == 0
    jb = d_s5 // LANES
    gb = LANES // S5_GROUP
    n_state = s5_a_re.shape[-1]
    sig = jax.nn.sigmoid
    o_u, o_q, o_i, o_f, o_g, o_ga = 0, d_s5, d_s5 + d_hg, d_s5 + 2 * d_hg, d_s5 + 4 * d_hg, d_s5 + 5 * d_hg

    h = x.reshape(n_tok, d_model)
    h_meta = meta_tokens.astype(x.dtype)
    for l in range(depth):
        assert l == 0, "the meta-token shortcut is only valid for a single layer"
        a = _rmsnorm(h, norm_mix_w[l])
        a_m = _rmsnorm(h_meta, norm_mix_w[l])
        w = w_in[l].astype(BF16)
        proj = functools.partial(_matmul, [(a, w)])
        proj_m = functools.partial(_matmul, [(a_m, w)], out_dtype=F32)

        u_ch = proj(n_cols=d_s5, col_off=o_u, lane_blocked=True, name="proj_u")
        q = proj(n_cols=d_hg, col_off=o_q, epilogue=jax.nn.silu, name="proj_q")
        v = proj(n_cols=d_hg, col_off=o_i, name="proj_i")
        z_fb = proj(n_cols=2 * d_hg, col_off=o_f, out_dtype=F32, name="proj_f")
        g_out = proj(n_cols=d_hg, col_off=o_g, epilogue=jax.nn.silu, name="proj_g")
        gates = proj(n_cols=2 * d_model, col_off=o_ga, epilogue=sig, name="proj_gates")
        u_m = proj_m(n_cols=d_s5, col_off=o_u, name="proj_u_meta")
        v_m = proj_m(n_cols=d_hg, col_off=o_i, out_dtype=BF16, name="proj_i_meta")
        zf_m = proj_m(n_cols=d_hg, col_off=o_f, name="proj_f_meta")

        ec, (kf, kb_rev), fc, lam_t = _s5_compact_operators(
            s5_a_re[l].astype(F32), s5_a_im[l].astype(F32), s5_log_dt[l].astype(F32), s5_b_re[l].astype(F32),
            s5_b_im[l].astype(F32), s5_c_re[l].astype(F32), s5_c_im[l].astype(F32), s5_d[l].astype(F32))
        rep_tq = _replicator(T, S5_GROUP, gb)
        rep_dp = _replicator(4, n_state, gb)
        e_mat = _expand(ec, rep_dp, S5_GROUP, n_state, gb, "s5_expand_e")
        m_mat = _expand_toeplitz(kf, kb_rev, rep_tq, T, S5_GROUP, gb, "s5_expand_m")
        f_mat = _expand(fc, rep_tq, n_state, S5_GROUP, gb, "s5_expand_f")
        sw = e_mat.shape[-1]
        um = u_m.reshape(T, jb, LANES).transpose(1, 0, 2).reshape(jb, 1, T * LANES)
        um = jnp.pad(um, ((0, 0), (0, 7), (0, 0))).astype(BF16)
        x0 = _matmul([(um, e_mat)], n_cols=sw, out_dtype=F32, name="s5_state_meta")[:, 0:1, :]
        x0 = jnp.concatenate([x0[..., :sw // 2], jnp.zeros_like(x0[..., sw // 2:])], axis=-1)
        s_all = _matmul([(u_ch, e_mat)], n_cols=sw, out_dtype=F32, name="s5_chunk_state")
        x_prev = _s5_scan(s_all, lam_t, x0, n_batch)
        y_ch = _matmul([(u_ch, m_mat), (x_prev, f_mat)], n_cols=T * LANES, name="s5_chunk_out")
        s5_out = _glu(y_ch, s5_glu_w[l].astype(BF16), s5_glu_b[l])

        lb = jnp.cumsum(jax.nn.softmax(hg_lb_gamma.astype(F32), axis=1), axis=1)[:, l]
        hg_out = _gla(q, v, z_fb, g_out, v_m, zf_m, lb, hg_norm_w[l], n_batch)

        mrg = _merge(s5_out, w_branch_a[l].astype(BF16), hg_out, w_branch_b[l].astype(BF16), gates)
        h = _matmul([(mrg, w_out[l].astype(BF16))], n_cols=d_model, extras=(h,),
                    epilogue=lambda acc, r: acc + r, out_dtype=F32, name="out_proj")

        ng, n_exp = router_group_w.shape[-1], router_expert_w.shape[-1]
        rw = jnp.concatenate([router_group_w[l], router_expert_w[l],
                              jnp.zeros((d_model, ROUTER_COLS - ng - n_exp), F32)], axis=1).astype(F32)
        rb = jnp.concatenate([router_group_b[l], router_expert_b[l],
                              jnp.zeros((ROUTER_COLS - ng - n_exp,), F32)]).reshape(1, ROUTER_COLS).astype(F32)
        assert (ng, n_exp) == (N_EXPERT_GROUPS, N_EXPERT_GROUPS * EXPERTS_PER_GROUP)
        f, route = _router(h, norm_ffn_w[l], rw, rb, ng, n_exp // ng)
        tm_c = _tile(n_tok, 128)
        blk_e, n_used, buf_tok, buf_w, pos = _routing(route, n_tok, tm_c)
        ys = _moe(f, expert_w1[l], expert_w3[l], expert_w2[l], blk_e, n_used, buf_tok, buf_w)
    out = _combine(h, ys, pos, norm_final_w)
    return out.reshape(n_batch, seq, d_model)
```
